```python
import math
import jax
import jax.numpy as jnp
from jax import lax
import numpy as np

D_MODEL = 2048
BATCH = 16
SEQ = 2048
DEPTH = 4

GRID_W = 64
CTX_LEN = 256
N_BRANCH = 4
BRANCH_W = D_MODEL // N_BRANCH
MIX_W = N_BRANCH * BRANCH_W
NORM_EPS = 1e-6
F32 = jnp.float32
S5_H = 16
S5_G = BRANCH_W // S5_H
S5_P = 64
HEAD_DIM = 64
ATT_HEADS = BRANCH_W // HEAD_DIM
ATT_KV_HEADS = 2
ATT_REP = ATT_HEADS // ATT_KV_HEADS
WINDOW = 128
BLOCK = 128
ROPE_BASE = 10000.0
NEG_INF = -1e30
RW_N = 64
RW_HEADS = BRANCH_W // RW_N
RW_LORA = 32
RW_SHIFT_W = 3 * BRANCH_W + 4 * RW_LORA
RW_LN_EPS = 64e-5
HY_ORDER = 2
HY_EMB = 33
HY_BANDS = (HY_EMB - 1) // 2
HY_FFN = 64
HY_N_FILT = 2 * HY_ORDER
HY_TARGET = 1e-2
HY_FAST_PCT = 0.3
HY_SLOW_PCT = 1.5
P_S5 = BRANCH_W
P_Q = ATT_HEADS * HEAD_DIM
P_KV = ATT_KV_HEADS * HEAD_DIM
P_RKV = 3 * BRANCH_W
P_LORA = 4 * RW_LORA
P_HY = (HY_ORDER + 1) * BRANCH_W
P_GATE = MIX_W
D_IN = P_S5 + P_Q + 2 * P_KV + P_RKV + P_LORA + P_HY + P_GATE

kernel_name = 'hybrid_parallel_heads_dit_block'


def rms_norm(x, g, eps=NORM_EPS):
    xf = x.astype(F32)
    y = xf * lax.rsqrt(jnp.mean(xf * xf, axis=-1, keepdims=True) + eps)
    return (y * g.astype(F32)).astype(x.dtype)


def centred_shift(z):
    prev = jnp.pad(z[:, :-1], ((0, 0), (1, 0), (0, 0)))
    nxt = jnp.pad(z[:, 1:], ((0, 0), (0, 1), (0, 0)))
    return prev, nxt


def centred_conv3(z, w, b):
    prev, nxt = centred_shift(z)
    return prev * w[0] + z * w[1] + nxt * w[2] + b


def split_proj(p):
    sizes = (P_S5, P_Q, P_KV, P_KV, P_RKV, P_LORA, P_HY, P_GATE)
    parts, start = [], 0
    for s in sizes:
        parts.append(p[..., start:start + s])
        start += s
    return parts


def s5_discretise(lam_re, lam_im, log_step, b_re, b_im):
    lam_re, lam_im = lam_re.astype(F32), lam_im.astype(F32)
    step = jnp.exp(log_step.astype(F32))[:, None]
    mag = jnp.exp(lam_re * step)
    lb_re, lb_im = mag * jnp.cos(lam_im * step), mag * jnp.sin(lam_im * step)
    den = lam_re * lam_re + lam_im * lam_im
    nr = lb_re - 1.0
    co_re = (nr * lam_re + lb_im * lam_im) / den
    co_im = (lb_im * lam_re - nr * lam_im) / den
    b_re, b_im = b_re.astype(F32), b_im.astype(F32)
    bb_re = co_re[..., None] * b_re - co_im[..., None] * b_im
    bb_im = co_re[..., None] * b_im + co_im[..., None] * b_re
    return lb_re, lb_im, bb_re, bb_im


def _ssm_combine(e1, e2):
    a1r, a1i, b1r, b1i = e1
    a2r, a2i, b2r, b2i = e2
    return (a2r * a1r - a2i * a1i, a2r * a1i + a2i * a1r,
            a2r * b1r - a2i * b1i + b2r, a2r * b1i + a2i * b1r + b2i)


def s5_states(lb_re, lb_im, bb_re, bb_im, u_tm, reverse, h0):
    if reverse:
        u_tm = u_tm[::-1]
    bu_re = jnp.einsum('lbgh,gph->lbgp', u_tm, bb_re)
    bu_im = jnp.einsum('lbgh,gph->lbgp', u_tm, bb_im)
    shape = (u_tm.shape[0], 1) + lb_re.shape
    a_re, a_im, h_re, h_im = lax.associative_scan(
        _ssm_combine,
        (jnp.broadcast_to(lb_re, shape), jnp.broadcast_to(lb_im, shape), bu_re, bu_im), axis=0)
    if h0 is not None:
        h_re, h_im = (h_re + a_re * h0[0] - a_im * h0[1],
                      h_im + a_re * h0[1] + a_im * h0[0])
    return h_re, h_im


def s5_readout(h, c_re, c_im, reverse):
    y = jnp.einsum('lbgp,ghp->lbgh', h[0], c_re) - jnp.einsum('lbgp,ghp->lbgh', h[1], c_im)
    return y[::-1] if reverse else y


def s5_mixer(u_lat, u_ctx, lam_re, lam_im, log_step, b_re, b_im, c_re, c_im, d_skip,
             glu_w, glu_b, ctx_out):
    dtype = u_lat.dtype

    def to_tm(u):
        return jnp.swapaxes(u.astype(F32).reshape(u.shape[0], u.shape[1], S5_G, S5_H), 0, 1)

    ul, uc = to_tm(u_lat), to_tm(u_ctx)
    d = d_skip.astype(F32)
    y_lat = d * ul
    y_ctx = d * uc if ctx_out else None
    for di, rev in enumerate((False, True)):
        lb_re, lb_im, bb_re, bb_im = s5_discretise(lam_re[di], lam_im[di], log_step[di],
                                                   b_re[di], b_im[di])
        cr, ci = c_re[di].astype(F32), c_im[di].astype(F32)
        hc = s5_states(lb_re, lb_im, bb_re, bb_im, uc, rev, None)
        hl = s5_states(lb_re, lb_im, bb_re, bb_im, ul, rev, (hc[0][-1], hc[1][-1]))
        y_lat = y_lat + s5_readout(hl, cr, ci, rev)
        if ctx_out:
            y_ctx = y_ctx + s5_readout(hc, cr, ci, rev)

    def glu(y_tm):
        y = jnp.swapaxes(y_tm, 0, 1)
        y = jax.nn.gelu(y.reshape(y.shape[0], y.shape[1], BRANCH_W), approximate=False)
        y = y * jax.nn.sigmoid(y @ glu_w.astype(F32) + glu_b.astype(F32))
        return y.astype(dtype)

    return glu(y_lat), (glu(y_ctx) if ctx_out else None)


def axial_rope(t, row, col):
    half = HEAD_DIM // 2
    quarter = half // 2
    inv = 1.0 / (ROPE_BASE ** (jnp.arange(quarter, dtype=F32) / quarter))

    def rot(u, pos):
        ang = pos.astype(F32)[:, None] * inv[None, :]
        cos, sin = jnp.cos(ang)[None, :, None, :], jnp.sin(ang)[None, :, None, :]
        u = u.astype(F32)
        u1, u2 = u[..., :quarter], u[..., quarter:]
        return jnp.concatenate([u1 * cos - u2 * sin, u2 * cos + u1 * sin], axis=-1)

    return jnp.concatenate([rot(t[..., :half], row), rot(t[..., half:], col)], axis=-1).astype(t.dtype)


def attn_mixer(q_l, k_l, v_l, q_c, k_c, v_c, q_g, k_g, sink, row, col, ctx_out):
    b_, n_, _ = q_l.shape
    n_ctx = k_c.shape[1]

    def heads(t, h):
        return t.reshape(t.shape[0], t.shape[1], h, HEAD_DIM)

    q = axial_rope(rms_norm(heads(q_l, ATT_HEADS), q_g), row, col)
    k = axial_rope(rms_norm(heads(k_l, ATT_KV_HEADS), k_g), row, col)
    v = heads(v_l, ATT_KV_HEADS)
    kc = rms_norm(heads(k_c, ATT_KV_HEADS), k_g)
    vc = heads(v_c, ATT_KV_HEADS)
    scale = HEAD_DIM ** -0.5
    sink_f = sink.astype(F32).reshape(ATT_KV_HEADS, ATT_REP)

    nb = n_ // BLOCK
    qb = q.reshape(b_, nb, BLOCK, ATT_KV_HEADS, ATT_REP, HEAD_DIM)

    def band(t):
        tp = jnp.pad(t, ((0, 0), (BLOCK, BLOCK), (0, 0), (0, 0)))
        tp = tp.reshape(b_, nb + 2, BLOCK, ATT_KV_HEADS, HEAD_DIM)
        return jnp.concatenate([tp[:, :-2], tp[:, 1:-1], tp[:, 2:]], axis=2)

    kw, vw = band(k), band(v)
    qpos = jnp.arange(nb)[:, None] * BLOCK + jnp.arange(BLOCK)[None, :]
    kpos = jnp.arange(nb)[:, None] * BLOCK - BLOCK + jnp.arange(3 * BLOCK)[None, :]
    valid = ((jnp.abs(qpos[:, :, None] - kpos[:, None, :]) <= WINDOW)
             & ((kpos >= 0) & (kpos < n_))[:, None, :])
    s_loc = jnp.einsum('bnqhrd,bnkhd->bnhrqk', qb, kw).astype(F32) * scale
    s_loc = jnp.where(valid[None, :, None, None], s_loc, NEG_INF)
    s_ctx = jnp.einsum('bnqhrd,bchd->bnhrqc', qb, kc).astype(F32) * scale
    s_sink = jnp.broadcast_to(sink_f[None, None, :, :, None, None], s_loc.shape[:-1] + (1,))
    p = jax.nn.softmax(jnp.concatenate([s_loc, s_ctx, s_sink], axis=-1), axis=-1).astype(v.dtype)
    o = (jnp.einsum('bnhrqk,bnkhd->bnqhrd', p[..., :3 * BLOCK], vw)
         + jnp.einsum('bnhrqc,bchd->bnqhrd', p[..., 3 * BLOCK:3 * BLOCK + n_ctx], vc))
    o_lat = o.reshape(b_, n_, ATT_HEADS * HEAD_DIM)

    o_ctx = None
    if ctx_out:
        qc = rms_norm(heads(q_c, ATT_HEADS), q_g).reshape(b_, n_ctx, ATT_KV_HEADS, ATT_REP, HEAD_DIM)
        s = jnp.einsum('bqhrd,bchd->bhrqc', qc, kc).astype(F32) * scale
        s_sink_c = jnp.broadcast_to(sink_f[None, :, :, None, None], s.shape[:-1] + (1,))
        pc = jax.nn.softmax(jnp.concatenate([s, s_sink_c], axis=-1), axis=-1).astype(vc.dtype)
        o_ctx = jnp.einsum('bhrqc,bchd->bqhrd', pc[..., :n_ctx], vc).reshape(b_, n_ctx, ATT_HEADS * HEAD_DIM)
    return o_lat, o_ctx


def rwkv_features(rkv, lora, mu_prev, mu_next):
    z = jnp.concatenate([rkv, lora], axis=-1).astype(F32)
    prev, nxt = centred_shift(z)
    z = z + mu_prev * (prev - z) + mu_next * (nxt - z)
    b_, l_, _ = z.shape

    def heads(t):
        return t.reshape(b_, l_, RW_HEADS, RW_N)

    r = heads(z[..., :BRANCH_W])
    k = heads(z[..., BRANCH_W:2 * BRANCH_W])
    v = heads(z[..., 2 * BRANCH_W:3 * BRANCH_W])
    return r, k, v, z[..., 3 * BRANCH_W:]


def rwkv_kk(k, k_k):
    kk = k * k_k
    return kk * lax.rsqrt(jnp.sum(kk * kk, axis=-1, keepdims=True) + 1e-12)


def rwkv_direction(k, kk, lora, di, w0, w2, a0, a2, k_a):
    b_, l_ = k.shape[:2]
    wl = lora[..., di * RW_LORA:(di + 1) * RW_LORA]
    al = lora[..., (2 + di) * RW_LORA:(3 + di) * RW_LORA]
    w_log = -jax.nn.softplus(-(w0[di].astype(F32) + jnp.tanh(wl) @ w2[di].astype(F32))) - 0.5
    decay = jnp.exp(-jnp.exp(w_log)).reshape(b_, l_, RW_HEADS, RW_N)
    a = jax.nn.sigmoid(a0[di].astype(F32) + al @ a2[di].astype(F32)).reshape(b_, l_, RW_HEADS, RW_N)
    return decay, k * (1.0 + (a - 1.0) * k_a), kk * a


def _rwkv_step(state, inp):
    r_t, w_t, k_t, v_t, kk_t, b_t = inp
    sa = jnp.einsum('bhij,bhj->bhi', state, -kk_t)
    state = (state * w_t[:, :, None, :] + sa[..., None] * b_t[:, :, None, :]
             + v_t[..., None] * k_t[:, :, None, :])
    return state, jnp.einsum('bhij,bhj->bhi', state, r_t)


def rwkv_run(r, decay, k, v, kk, b, s0, reverse):
    def tm(t):
        t = jnp.swapaxes(t, 0, 1)
        return t[::-1] if reverse else t

    s_fin, y = lax.scan(_rwkv_step, s0, (tm(r), tm(decay), tm(k), tm(v), tm(kk), tm(b)))
    y = y[::-1] if reverse else y
    return s_fin, jnp.swapaxes(y, 0, 1)


def rwkv_mixer(rkv_l, lora_l, rkv_c, lora_c, mu_prev, mu_next, w0, w2, a0, a2,
               k_k, k_a, r_k, ln_g, ln_b, ctx_out):
    dtype = rkv_l.dtype
    mu_prev, mu_next = mu_prev.astype(F32), mu_next.astype(F32)
    k_k = k_k.astype(F32).reshape(RW_HEADS, RW_N)
    k_a = k_a.astype(F32).reshape(RW_HEADS, RW_N)
    r_k = r_k.astype(F32)
    lat = rwkv_features(rkv_l, lora_l, mu_prev, mu_next)
    ctx = rwkv_features(rkv_c, lora_c, mu_prev, mu_next)
    kk_l, kk_c = rwkv_kk(lat[1], k_k), rwkv_kk(ctx[1], k_k)
    b_ = rkv_l.shape[0]
    y_l, bonus_l = jnp.zeros_like(lat[0]), jnp.zeros_like(lat[0][..., :1])
    y_c, bonus_c = jnp.zeros_like(ctx[0]), jnp.zeros_like(ctx[0][..., :1])
    for di, rev in enumerate((False, True)):
        dec_c, kd_c, b_c = rwkv_direction(ctx[1], kk_c, ctx[3], di, w0, w2, a0, a2, k_a)
        dec_l, kd_l, b_l = rwkv_direction(lat[1], kk_l, lat[3], di, w0, w2, a0, a2, k_a)
        s0 = jnp.zeros((b_, RW_HEADS, RW_N, RW_N), F32)
        s_c, yc = rwkv_run(ctx[0], dec_c, kd_c, ctx[2], kk_c, b_c, s0, rev)
        _, yl = rwkv_run(lat[0], dec_l, kd_l, lat[2], kk_l, b_l, s_c, rev)
        y_l = y_l + yl
        bonus_l = bonus_l + jnp.sum(lat[0] * kd_l * r_k, axis=-1, keepdims=True)
        if ctx_out:
            y_c = y_c + yc
            bonus_c = bonus_c + jnp.sum(ctx[0] * kd_c * r_k, axis=-1, keepdims=True)

    def finish(y, bonus, v):
        mu = jnp.mean(y, axis=-1, keepdims=True)
        var = jnp.mean(jnp.square(y - mu), axis=-1, keepdims=True)
        y = (y - mu) * lax.rsqrt(var + RW_LN_EPS)
        bb, ll = y.shape[:2]
        y = y.reshape(bb, ll, BRANCH_W) * ln_g.astype(F32) + ln_b.astype(F32)
        return (y + (bonus * v).reshape(bb, ll, BRANCH_W)).astype(dtype)

    return finish(y_l, bonus_l, lat[2]), (finish(y_c, bonus_c, ctx[2]) if ctx_out else None)


def hyena_filters(n, w1, b1, f1, w2, b2, f2, w3):
    t = jnp.linspace(0.0, 1.0, n, dtype=F32)[:, None]
    ang = 2.0 * math.pi * jnp.arange(n, dtype=F32)[:, None] / n
    bands = jnp.linspace(1e-4, HY_BANDS - 1, HY_BANDS, dtype=F32)[None, :]
    z = jnp.concatenate([t, jnp.cos(bands * ang), -jnp.sin(bands * ang)], axis=-1)
    h = jnp.sin(f1.astype(F32) * (z @ w1.astype(F32) + b1.astype(F32)))
    h = jnp.sin(f2.astype(F32) * (h @ w2.astype(F32) + b2.astype(F32)))
    h = (h @ w3.astype(F32)).reshape(n, HY_N_FILT, BRANCH_W)
    deltas = jnp.abs(jnp.linspace(math.log(HY_TARGET) / HY_SLOW_PCT, math.log(HY_TARGET) / HY_FAST_PCT,
                                  BRANCH_W, dtype=F32))
    return h * jnp.exp(-t[:, :, None] * deltas)


def bidir_fftconv(u, h_f, h_b, skip):
    n = u.shape[1]
    g = jnp.concatenate([h_f, jnp.zeros_like(h_f[:1]), h_b[1:][::-1]], axis=0)
    gf = jnp.fft.rfft(g, axis=0)
    uf = jnp.fft.rfft(u, n=2 * n, axis=1)
    y = jnp.fft.irfft(uf * gf[None], n=2 * n, axis=1)[:, :n]
    return y + skip * u


def hyena_seq(z, conv_w, conv_b, w1, b1, f1, w2, b2, f2, w3, skip):
    dtype = z.dtype
    z = centred_conv3(z.astype(F32), conv_w.astype(F32), conv_b.astype(F32))
    v, x1, x2 = z[..., :BRANCH_W], z[..., BRANCH_W:2 * BRANCH_W], z[..., 2 * BRANCH_W:]
    h = hyena_filters(z.shape[1], w1, b1, f1, w2, b2, f2, w3)
    skip = skip.astype(F32)
    y = x1 * bidir_fftconv(v, h[:, 0], h[:, 1], skip[0])
    y = x2 * bidir_fftconv(y, h[:, 2], h[:, 3], skip[1])
    return y.astype(dtype)


def merge_heads(y_s5, y_att, y_rw, y_hy, gate_pre, branch_g, w_out):
    y = jnp.concatenate([rms_norm(y_s5, branch_g[0]), rms_norm(y_att, branch_g[1]), y_rw,
                         rms_norm(y_hy, branch_g[2])], axis=-1)
    return (y * jax.nn.silu(gate_pre)) @ w_out


def setup_inputs(seed: int = 0) -> dict:
    key = jax.random.key(seed)
    keys = iter(jax.random.split(key, 64))
    W = BRANCH_W

    def nrm(shape, std):
        return std * jax.random.normal(next(keys), shape, F32)

    def uni(shape, lo, hi):
        return jax.random.uniform(next(keys), shape, F32, lo, hi)

    s5_n = jnp.arange(S5_P, dtype=F32)
    return {
        'x': nrm((BATCH, SEQ, D_MODEL), 1.0),
        'c': nrm((BATCH, D_MODEL), 1.0),
        'ctx': nrm((BATCH, CTX_LEN, D_MODEL), 1.0),
        'c_ctx': nrm((D_MODEL,), 1.0),
        'norm_g': 1.0 + nrm((DEPTH, D_MODEL), 0.02),
        'w_ada': nrm((DEPTH, D_MODEL, 3 * D_MODEL), 0.5 * D_MODEL ** -0.5),
        'b_ada': nrm((DEPTH, 3 * D_MODEL), 0.02),
        'w_in': nrm((DEPTH, D_MODEL, D_IN), D_MODEL ** -0.5),
        'w_out': nrm((DEPTH, MIX_W, D_MODEL), MIX_W ** -0.5),
        'branch_g': 1.0 + nrm((DEPTH, 3, W), 0.02),
        's5_lam_re': -0.5 + nrm((DEPTH, 2, S5_G, S5_P), 0.01),
        's5_lam_im': math.pi * s5_n + nrm((DEPTH, 2, S5_G, S5_P), 0.01),
        's5_log_step': uni((DEPTH, 2, S5_G), math.log(1e-3), math.log(1e-1)),
        's5_b_re': nrm((DEPTH, 2, S5_G, S5_P, S5_H), (2 * S5_H) ** -0.5),
        's5_b_im': nrm((DEPTH, 2, S5_G, S5_P, S5_H), (2 * S5_H) ** -0.5),
        's5_c_re': nrm((DEPTH, 2, S5_G, S5_H, S5_P), (2 * S5_P) ** -0.5),
        's5_c_im': nrm((DEPTH, 2, S5_G, S5_H, S5_P), (2 * S5_P) ** -0.5),
        's5_d': nrm((DEPTH, S5_G, S5_H), 1.0),
        's5_glu_w': nrm((DEPTH, W, W), W ** -0.5),
        's5_glu_b': nrm((DEPTH, W), 0.02),
        'att_q_g': 1.0 + nrm((DEPTH, HEAD_DIM), 0.02),
        'att_k_g': 1.0 + nrm((DEPTH, HEAD_DIM), 0.02),
        'att_sink': nrm((DEPTH, ATT_HEADS), 0.5),
        'rw_mu_prev': uni((DEPTH, RW_SHIFT_W), 0.0, 0.5),
        'rw_mu_next': uni((DEPTH, RW_SHIFT_W), 0.0, 0.5),
        'rw_w0': jnp.linspace(-6.0, -1.0, W, dtype=F32) + nrm((DEPTH, 2, W), 0.1),
        'rw_w2': nrm((DEPTH, 2, RW_LORA, W), 0.1),
        'rw_a0': nrm((DEPTH, 2, W), 0.1),
        'rw_a2': nrm((DEPTH, 2, RW_LORA, W), 0.1),
        'rw_k_k': 0.85 + nrm((DEPTH, W), 0.02),
        'rw_k_a': 1.0 + nrm((DEPTH, W), 0.02),
        'rw_r_k': nrm((DEPTH, RW_HEADS, RW_N), 0.1),
        'rw_ln_g': 1.0 + nrm((DEPTH, W), 0.02),
        'rw_ln_b': nrm((DEPTH, W), 0.02),
        'hy_conv_w': nrm((DEPTH, 3, 3 * W), 3 ** -0.5),
        'hy_conv_b': nrm((DEPTH, 3 * W), 0.02),
        'hy_w1': nrm((DEPTH, HY_EMB, HY_FFN), HY_EMB ** -0.5),
        'hy_b1': nrm((DEPTH, HY_FFN), 0.1),
        'hy_f1': 1.0 + nrm((DEPTH, HY_FFN), 0.05),
        'hy_w2': nrm((DEPTH, HY_FFN, HY_FFN), HY_FFN ** -0.5),
        'hy_b2': nrm((DEPTH, HY_FFN), 0.1),
        'hy_f2': 1.0 + nrm((DEPTH, HY_FFN), 0.05),
        'hy_w3': nrm((DEPTH, HY_FFN, HY_N_FILT * W), HY_FFN ** -0.5),
        'hy_skip': nrm((DEPTH, 2, W), 0.5),
    }


def reference(x, c, ctx, c_ctx, norm_g, w_ada, b_ada, w_in, w_out, branch_g,
              s5_lam_re, s5_lam_im, s5_log_step, s5_b_re, s5_b_im, s5_c_re, s5_c_im, s5_d,
              s5_glu_w, s5_glu_b, att_q_g, att_k_g, att_sink,
              rw_mu_prev, rw_mu_next, rw_w0, rw_w2, rw_a0, rw_a2, rw_k_k, rw_k_a, rw_r_k,
              rw_ln_g, rw_ln_b, hy_conv_w, hy_conv_b, hy_w1, hy_b1, hy_f1, hy_w2, hy_b2, hy_f2,
              hy_w3, hy_skip):
    n_tok = x.shape[1]
    rows = n_tok // GRID_W
    row = jnp.repeat(jnp.arange(rows, dtype=jnp.int32), GRID_W)
    col = jnp.tile(jnp.arange(GRID_W, dtype=jnp.int32), rows)
    xc = ctx
    silu_c, silu_cc = jax.nn.silu(c), jax.nn.silu(c_ctx)
    for l in range(DEPTH):
        ctx_out = l < DEPTH - 1
        mod_x = silu_c @ w_ada[l] + b_ada[l]
        shift_x, scale_x, gate_x = jnp.split(mod_x[:, None, :], 3, axis=-1)
        mod_c = silu_cc @ w_ada[l] + b_ada[l]
        shift_c, scale_c, gate_c = jnp.split(mod_c, 3)
        hx = rms_norm(x, norm_g[l]) * (1.0 + scale_x) + shift_x
        hc = rms_norm(xc, norm_g[l]) * (1.0 + scale_c) + shift_c
        s5u_l, q_l, k_l, v_l, rkv_l, lora_l, hy_l, g_l = split_proj(hx @ w_in[l])
        s5u_c, q_c, k_c, v_c, rkv_c, lora_c, hy_c, g_c = split_proj(hc @ w_in[l])

        y_s5_l, y_s5_c = s5_mixer(s5u_l, s5u_c, s5_lam_re[l], s5_lam_im[l], s5_log_step[l],
                                  s5_b_re[l], s5_b_im[l], s5_c_re[l], s5_c_im[l], s5_d[l],
                                  s5_glu_w[l], s5_glu_b[l], ctx_out)
        y_att_l, y_att_c = attn_mixer(q_l, k_l, v_l, q_c, k_c, v_c, att_q_g[l], att_k_g[l],
                                      att_sink[l], row, col, ctx_out)
        y_rw_l, y_rw_c = rwkv_mixer(rkv_l, lora_l, rkv_c, lora_c, rw_mu_prev[l], rw_mu_next[l],
                                    rw_w0[l], rw_w2[l], rw_a0[l], rw_a2[l], rw_k_k[l], rw_k_a[l],
                                    rw_r_k[l], rw_ln_g[l], rw_ln_b[l], ctx_out)
        y_hy_l = hyena_seq(hy_l, hy_conv_w[l], hy_conv_b[l], hy_w1[l], hy_b1[l], hy_f1[l],
                           hy_w2[l], hy_b2[l], hy_f2[l], hy_w3[l], hy_skip[l])
        x = x + gate_x * merge_heads(y_s5_l, y_att_l, y_rw_l, y_hy_l, g_l, branch_g[l], w_out[l])
        if ctx_out:
            y_hy_c = hyena_seq(hy_c, hy_conv_w[l], hy_conv_b[l], hy_w1[l], hy_b1[l], hy_f1[l],
                               hy_w2[l], hy_b2[l], hy_f2[l], hy_w3[l], hy_skip[l])
            xc = xc + gate_c * merge_heads(y_s5_c, y_att_c, y_rw_c, y_hy_c, g_c, branch_g[l], w_out[l])
    return x
```

```python
import functools
import math

import jax
import jax.numpy as jnp
from jax import lax
from jax.experimental import pallas as pl
from jax.experimental.pallas import tpu as pltpu

F32 = jnp.float32
BF16 = jnp.bfloat16
HI = lax.Precision.HIGHEST

D_MODEL = 2048
GRID_W = 64
BRANCH_W = 512
NORM_EPS = 1e-6
S5_H = 16
S5_G = BRANCH_W // S5_H
S5_P = 64
HEAD_DIM = 64
ATT_HEADS = 8
ATT_KV_HEADS = 2
ATT_REP = ATT_HEADS // ATT_KV_HEADS
WINDOW = 128
BLOCK = 128
ROPE_BASE = 10000.0
NEG_INF = -1e30
RW_N = 64
RW_HEADS = 8
RW_LORA = 32
RW_LN_EPS = 64e-5
HY_EMB = 33
HY_BANDS = 16
HY_FFN = 64
HY_TARGET = 1e-2
HY_FAST_PCT = 0.3
HY_SLOW_PCT = 1.5
P_S5 = BRANCH_W
P_QKV = ATT_HEADS * HEAD_DIM + 2 * ATT_KV_HEADS * HEAD_DIM
P_RW = 3 * BRANCH_W + 4 * RW_LORA
P_HY = 3 * BRANCH_W
P_GATE = 4 * BRANCH_W

V7X_VMEM_BYTES = 64 * 1024 * 1024
S5_T = 16
RW_C = 64
ROW_TILE = 256
MOD_ROWS = 24


def _cparams(sem, vmem_mb):
    assert vmem_mb * 1024 * 1024 <= V7X_VMEM_BYTES
    return pltpu.CompilerParams(dimension_semantics=sem, vmem_limit_bytes=vmem_mb * 1024 * 1024)


def _dot(a, b, precision=None):
    return jnp.dot(a, b, precision=precision, preferred_element_type=F32)


def _dot_nt(a, b, precision=None):
    return lax.dot_general(a, b, (((1,), (1,)), ((), ())), precision=precision, preferred_element_type=F32)


def _dot_tn(a, b, precision=None):
    return lax.dot_general(a, b, (((0,), (0,)), ((), ())), precision=precision, preferred_element_type=F32)


def _segsum(x, j):
    hi = x.astype(BF16)
    lo = (x - hi.astype(F32)).astype(BF16)
    return _dot(hi, j) + _dot(lo, j)


def _shift_rows(z, prev_row, next_row, zero_prev, zero_next):
    tm = z.shape[0]
    rowi = lax.broadcasted_iota(jnp.int32, (tm, 1), 0)
    prev = jnp.where(rowi == 0, prev_row, pltpu.roll(z, 1, 0))
    nxt = jnp.where(rowi == tm - 1, next_row, pltpu.roll(z, tm - 1, 0))
    prev = jnp.where(zero_prev, 0.0, prev)
    nxt = jnp.where(zero_next, 0.0, nxt)
    return prev, nxt


def _mod_kernel(c_ref, w_ref, b_ref, o_ref):
    c = c_ref[...]
    s = c * jax.nn.sigmoid(c)
    o_ref[0] = _dot(s, w_ref[0], HI) + b_ref[0]


def _ada_mod(c_all, w_ada, b_ada):
    depth, d, d3 = w_ada.shape
    tn = 512
    return pl.pallas_call(
        _mod_kernel,
        grid=(depth, d3 // tn),
        in_specs=[pl.BlockSpec((MOD_ROWS, d), lambda l, j: (0, 0)),
                  pl.BlockSpec((1, d, tn), lambda l, j: (l, 0, j)),
                  pl.BlockSpec((1, 1, tn), lambda l, j: (l, 0, j))],
        out_specs=pl.BlockSpec((1, MOD_ROWS, tn), lambda l, j: (l, 0, j)),
        out_shape=jax.ShapeDtypeStruct((depth, MOD_ROWS, d3), F32),
        compiler_params=_cparams(("parallel", "parallel"), 32),
        name="ada_mod",
    )(c_all, w_ada, b_ada.reshape(depth, 1, d3))


def _norm_mod_kernel(x_ref, g_ref, shx_ref, scx_ref, shc_ref, scc_ref, o_ref, *, n_ctx_blocks):
    x = x_ref[0]
    ms = jnp.mean(x * x, axis=-1, keepdims=True)
    y = x * lax.rsqrt(ms + NORM_EPS) * g_ref[...]
    is_ctx = pl.program_id(1) < n_ctx_blocks
    sc = jnp.where(is_ctx, scc_ref[0, 0], scx_ref[0, 0])
    sh = jnp.where(is_ctx, shc_ref[0, 0], shx_ref[0, 0])
    o_ref[0] = (y * (1.0 + sc) + sh).astype(BF16)


def _norm_mod(x_all, g, mod_l, n_ctx):
    bsz, n, d = x_all.shape
    tm = ROW_TILE
    ctx_row = bsz
    return pl.pallas_call(
        functools.partial(_norm_mod_kernel, n_ctx_blocks=n_ctx // tm),
        grid=(bsz, n // tm),
        in_specs=[pl.BlockSpec((1, tm, d), lambda b, j: (b, j, 0)),
                  pl.BlockSpec((1, d), lambda b, j: (0, 0)),
                  pl.BlockSpec((1, 1, 1, d), lambda b, j: (b, 0, 0, 0)),
                  pl.BlockSpec((1, 1, 1, d), lambda b, j: (b, 1, 0, 0)),
                  pl.BlockSpec((1, 1, 1, d), lambda b, j: (ctx_row, 0, 0, 0)),
                  pl.BlockSpec((1, 1, 1, d), lambda b, j: (ctx_row, 1, 0, 0))],
        out_specs=pl.BlockSpec((1, tm, d), lambda b, j: (b, j, 0)),
        out_shape=jax.ShapeDtypeStruct((bsz, n, d), BF16),
        compiler_params=_cparams(("parallel", "parallel"), 32),
        name="norm_mod",
    )(x_all, g.reshape(1, d), mod_l, mod_l, mod_l, mod_l)


def _mm_kernel(a_ref, b_ref, o_ref):
    o_ref[...] = _dot(a_ref[...], b_ref[...]).astype(o_ref.dtype)


def _mm(a, b, tm, tn):
    m, k = a.shape
    n = b.shape[1]
    assert m % tm == 0 and n % tn == 0
    return pl.pallas_call(
        _mm_kernel,
        grid=(n // tn, m // tm),
        in_specs=[pl.BlockSpec((tm, k), lambda j, i: (i, 0)),
                  pl.BlockSpec((k, tn), lambda j, i: (0, j))],
        out_specs=pl.BlockSpec((tm, tn), lambda j, i: (i, j)),
        out_shape=jax.ShapeDtypeStruct((m, n), F32),
        compiler_params=_cparams(("parallel", "parallel"), 48),
        name="in_proj",
    )(a, b)


def _s5_weights(lam_re, lam_im, log_step, b_re, b_im, c_re, c_im, n_scan_steps):
    t_len = S5_T
    step = jnp.exp(log_step)[..., None]
    th_re, th_im = lam_re * step, lam_im * step

    def cpow(k):
        mag = jnp.exp(th_re[..., None] * k)
        ang = th_im[..., None] * k
        return mag * jnp.cos(ang), mag * jnp.sin(ang)

    lb_re, lb_im = (t[..., 0] for t in cpow(jnp.ones((1,), F32)))
    den = lam_re * lam_re + lam_im * lam_im
    nr = lb_re - 1.0
    co_re = (nr * lam_re + lb_im * lam_im) / den
    co_im = (lb_im * lam_re - nr * lam_im) / den
    bb_re = co_re[..., None] * b_re - co_im[..., None] * b_im
    bb_im = co_re[..., None] * b_im + co_im[..., None] * b_re

    lags = jnp.arange(t_len + 1, dtype=F32)
    pw_re, pw_im = cpow(lags)
    x_re = c_re[..., None] * pw_re[:, :, None] - c_im[..., None] * pw_im[:, :, None]
    x_im = c_re[..., None] * pw_im[:, :, None] + c_im[..., None] * pw_re[:, :, None]
    m_k = (jnp.einsum('dgopk,dgpi->dgkio', x_re, bb_re, precision=HI)
           - jnp.einsum('dgopk,dgpi->dgkio', x_im, bb_im, precision=HI))
    s_idx = jnp.arange(t_len)[:, None]
    t_idx = jnp.arange(t_len)[None, :]
    lag_f = t_idx - s_idx
    kin = []
    for d, lag in enumerate((lag_f, -lag_f)):
        blk = m_k[d][:, jnp.clip(lag, 0, t_len)]
        blk = jnp.where((lag >= 0)[None, :, :, None, None], blk, 0.0)
        kin.append(blk.transpose(0, 1, 3, 2, 4).reshape(S5_G, t_len * S5_H, t_len * S5_H))
    kin = jnp.stack(kin)

    tt = jnp.arange(t_len)
    win, wout = [], []
    for d in range(2):
        e_in = (t_len - 1 - tt) if d == 0 else tt
        e_out = (tt + 1) if d == 0 else (t_len - tt)
        pr, pi = pw_re[d][..., e_in], pw_im[d][..., e_in]
        wr = pr[..., None] * bb_re[d][:, :, None] - pi[..., None] * bb_im[d][:, :, None]
        wi = pr[..., None] * bb_im[d][:, :, None] + pi[..., None] * bb_re[d][:, :, None]
        w = jnp.concatenate([wr, wi], axis=1)
        win.append(w.transpose(0, 2, 3, 1).reshape(S5_G, t_len * S5_H, 2 * S5_P))
        qr, qi = pw_re[d][..., e_out], pw_im[d][..., e_out]
        orr = c_re[d].transpose(0, 2, 1)[:, :, None] * qr[..., None] - c_im[d].transpose(0, 2, 1)[:, :, None] * qi[..., None]
        oii = -(c_re[d].transpose(0, 2, 1)[:, :, None] * qi[..., None] + c_im[d].transpose(0, 2, 1)[:, :, None] * qr[..., None])
        o = jnp.concatenate([orr, oii], axis=1)
        wout.append(o.reshape(S5_G, 2 * S5_P, t_len * S5_H))
    win, wout = jnp.stack(win), jnp.stack(wout)

    sc_re, sc_im = cpow(t_len * (2.0 ** jnp.arange(n_scan_steps, dtype=F32)))
    a1 = jnp.concatenate([sc_re, sc_re], axis=2)
    a2 = jnp.concatenate([-sc_im, sc_im], axis=2)
    pw = jnp.stack([a1, a2], axis=-1).transpose(0, 1, 3, 4, 2).reshape(2, S5_G, 2 * n_scan_steps, 2 * S5_P)
    return kin.astype(BF16), win.astype(BF16), wout.astype(BF16), pw


def _s5_kernel(x_ref, kin_ref, win_ref, wout_ref, pw_ref, y_ref, sa, sb, *, rows, bsz, n_steps):
    x = x_ref[0, 0]
    sa[...] = _dot(x, win_ref[0, 0])
    src, dst = sa, sb
    for i in range(n_steps):
        sh = (1 << i) * bsz
        a1 = pw_ref[0, 0, 2 * i:2 * i + 1, :]
        a2 = pw_ref[0, 0, 2 * i + 1:2 * i + 2, :]
        prev = src[0:rows - sh, :]
        dst[0:sh, :] = src[0:sh, :]
        dst[sh:rows, :] = src[sh:rows, :] + a1 * prev + a2 * pltpu.roll(prev, S5_P, 1)
        src, dst = dst, src
    y_ref[0, 0] = _dot(x, kin_ref[0, 0])
    h_start = src[0:rows - bsz, :].astype(BF16)
    y_ref[0, 0, bsz:rows, :] += _dot(h_start, wout_ref[0, 0])


def _s5_mix(u, weights, n_ctx):
    kin, win, wout, pw = weights
    bsz, n, _ = u.shape
    nc, nc_ctx = n // S5_T, n_ctx // S5_T
    th = S5_T * S5_H
    rows = nc * bsz
    n_steps = pw.shape[2] // 2
    assert (1 << n_steps) >= nc and bsz % 8 == 0
    x0 = u.reshape(bsz, nc, S5_T, S5_G, S5_H).transpose(3, 1, 0, 2, 4).reshape(S5_G, nc, bsz, th)

    def scan_order(z):
        return jnp.concatenate([z[:, :nc_ctx][:, ::-1], z[:, nc_ctx:][:, ::-1]], axis=1)

    x = jnp.stack([x0, scan_order(x0)]).reshape(2, S5_G, rows, th).astype(BF16)
    y = pl.pallas_call(
        functools.partial(_s5_kernel, rows=rows, bsz=bsz, n_steps=n_steps),
        grid=(2, S5_G),
        in_specs=[pl.BlockSpec((1, 1, rows, th), lambda d, g: (d, g, 0, 0)),
                  pl.BlockSpec((1, 1, th, th), lambda d, g: (d, g, 0, 0)),
                  pl.BlockSpec((1, 1, th, 2 * S5_P), lambda d, g: (d, g, 0, 0)),
                  pl.BlockSpec((1, 1, 2 * S5_P, th), lambda d, g: (d, g, 0, 0)),
                  pl.BlockSpec((1, 1, 2 * n_steps, 2 * S5_P), lambda d, g: (d, g, 0, 0))],
        out_specs=pl.BlockSpec((1, 1, rows, th), lambda d, g: (d, g, 0, 0)),
        out_shape=jax.ShapeDtypeStruct((2, S5_G, rows, th), F32),
        scratch_shapes=[pltpu.VMEM((rows, 2 * S5_P), F32), pltpu.VMEM((rows, 2 * S5_P), F32)],
        compiler_params=_cparams(("parallel", "parallel"), 32),
        name="s5_chunks",
    )(x, kin, win, wout, pw)
    y = y.reshape(2, S5_G, nc, bsz, S5_T, S5_H)

    def back(z):
        return z.transpose(2, 1, 3, 0, 4).reshape(bsz, n, BRANCH_W)

    return back(y[0]), back(scan_order(y[1]))


def _rope_tables(n_ctx, n_lat):
    quarter = HEAD_DIM // 4
    inv = 1.0 / (ROPE_BASE ** (jnp.arange(quarter, dtype=F32) / quarter))
    t = jnp.arange(n_lat, dtype=jnp.int32)
    ar = (t // GRID_W).astype(F32)[:, None] * inv[None, :]
    ac = (t % GRID_W).astype(F32)[:, None] * inv[None, :]
    cos = jnp.concatenate([jnp.cos(ar), jnp.cos(ar), jnp.cos(ac), jnp.cos(ac)], axis=1)
    sin = jnp.concatenate([-jnp.sin(ar), jnp.sin(ar), -jnp.sin(ac), jnp.sin(ac)], axis=1)
    cos = jnp.concatenate([jnp.ones((n_ctx, HEAD_DIM), F32), cos], axis=0)
    sin = jnp.concatenate([jnp.zeros((n_ctx, HEAD_DIM), F32), sin], axis=0)
    return jnp.concatenate([cos, cos], axis=1), jnp.concatenate([sin, sin], axis=1)


def _qk_prep_kernel(x_ref, cos_ref, sin_ref, qg_ref, kg_ref, j_ref, q_out, k_out, v_out):
    x = x_ref[0]
    wq, wk = ATT_HEADS * HEAD_DIM, ATT_KV_HEADS * HEAD_DIM
    cos, sin = cos_ref[...], sin_ref[...]

    def prep(t, g, w, scale):
        ss = _segsum(t * t, j_ref[0:w, 0:w])
        tn = t * lax.rsqrt(ss * (1.0 / HEAD_DIM) + NORM_EPS) * g
        reps = w // 128
        c = jnp.concatenate([cos] * reps, axis=1) if reps > 1 else cos
        s = jnp.concatenate([sin] * reps, axis=1) if reps > 1 else sin
        lane = lax.broadcasted_iota(jnp.int32, tn.shape, 1)
        first = (lane % 32) < 16
        swapped = jnp.where(first, pltpu.roll(tn, w - 16, 1), pltpu.roll(tn, 16, 1))
        return ((tn * c + swapped * s) * scale).astype(BF16)

    q = prep(x[:, 0:wq], qg_ref[...], wq, HEAD_DIM ** -0.5)
    k = prep(x[:, wq:wq + wk], kg_ref[...], wk, 1.0)
    v = x[:, wq + wk:wq + 2 * wk].astype(BF16)
    for h in range(ATT_HEADS):
        q_out[0, h] = q[:, h * HEAD_DIM:(h + 1) * HEAD_DIM]
    for h in range(ATT_KV_HEADS):
        k_out[0, h] = k[:, h * HEAD_DIM:(h + 1) * HEAD_DIM]
        v_out[0, h] = v[:, h * HEAD_DIM:(h + 1) * HEAD_DIM]


def _qk_prep(qkv, cos, sin, q_g, k_g, seg_j):
    bsz, n, w = qkv.shape
    tm = ROW_TILE
    wq, wk = ATT_HEADS * HEAD_DIM, ATT_KV_HEADS * HEAD_DIM
    return pl.pallas_call(
        _qk_prep_kernel,
        grid=(bsz, n // tm),
        in_specs=[pl.BlockSpec((1, tm, w), lambda b, j: (b, j, 0)),
                  pl.BlockSpec((tm, 128), lambda b, j: (j, 0)),
                  pl.BlockSpec((tm, 128), lambda b, j: (j, 0)),
                  pl.BlockSpec((1, wq), lambda b, j: (0, 0)),
                  pl.BlockSpec((1, wk), lambda b, j: (0, 0)),
                  pl.BlockSpec((BRANCH_W, BRANCH_W), lambda b, j: (0, 0))],
        out_specs=[pl.BlockSpec((1, ATT_HEADS, tm, HEAD_DIM), lambda b, j: (b, 0, j, 0)),
                   pl.BlockSpec((1, ATT_KV_HEADS, tm, HEAD_DIM), lambda b, j: (b, 0, j, 0)),
                   pl.BlockSpec((1, ATT_KV_HEADS, tm, HEAD_DIM), lambda b, j: (b, 0, j, 0))],
        out_shape=[jax.ShapeDtypeStruct((bsz, ATT_HEADS, n, HEAD_DIM), BF16),
                   jax.ShapeDtypeStruct((bsz, ATT_KV_HEADS, n, HEAD_DIM), BF16),
                   jax.ShapeDtypeStruct((bsz, ATT_KV_HEADS, n, HEAD_DIM), BF16)],
        compiler_params=_cparams(("parallel", "parallel"), 32),
        name="qk_prep",
    )(qkv, cos, sin, jnp.tile(q_g, ATT_HEADS).reshape(1, wq), jnp.tile(k_g, ATT_KV_HEADS).reshape(1, wk), seg_j)


def _attn_kernel(sink_ref, q_ref, k_ref, v_ref, o_ref, *, n_ctx, n_lat):
    kvh = pl.program_id(1)
    i = pl.program_id(2)
    n_ctx_blocks = n_ctx // BLOCK
    rows = ATT_REP * BLOCK
    span = 3 * BLOCK
    q = q_ref[0].reshape(rows, HEAD_DIM)
    kc = k_ref[0, 0, 0:n_ctx, :]
    vc = v_ref[0, 0, 0:n_ctx, :]
    s_ctx = _dot_nt(q, kc)
    rowi = lax.broadcasted_iota(jnp.int32, (rows, 1), 0)
    rep = rowi // BLOCK
    sink = jnp.zeros((rows, 1), F32)
    for r in range(ATT_REP):
        sink = jnp.where(rep == r, sink_ref[kvh * ATT_REP + r], sink)

    def store(o):
        o_ref[0] = jnp.concatenate([o[r * BLOCK:(r + 1) * BLOCK] for r in range(ATT_REP)], axis=1)

    @pl.when(i < n_ctx_blocks)
    def _():
        m = jnp.maximum(jnp.max(s_ctx, axis=1, keepdims=True), sink)
        p = jnp.exp(s_ctx - m)
        den = jnp.sum(p, axis=1, keepdims=True) + jnp.exp(sink - m)
        store(_dot(p.astype(BF16), vc) / den)

    @pl.when(i >= n_ctx_blocks)
    def _():
        il = i - n_ctx_blocks
        start = pl.multiple_of(jnp.clip((il - 1) * BLOCK, 0, n_lat - span), BLOCK)
        kw = k_ref[0, 0, pl.ds(n_ctx + start, span), :]
        vw = v_ref[0, 0, pl.ds(n_ctx + start, span), :]
        s_loc = _dot_nt(q, kw)
        qpos = il * BLOCK + rowi % BLOCK
        kpos = start + lax.broadcasted_iota(jnp.int32, (rows, span), 1)
        s_loc = jnp.where(jnp.abs(qpos - kpos) <= WINDOW, s_loc, NEG_INF)
        m = jnp.maximum(jnp.maximum(jnp.max(s_loc, axis=1, keepdims=True), jnp.max(s_ctx, axis=1, keepdims=True)), sink)
        p_loc = jnp.exp(s_loc - m)
        p_ctx = jnp.exp(s_ctx - m)
        den = jnp.sum(p_loc, axis=1, keepdims=True) + jnp.sum(p_ctx, axis=1, keepdims=True) + jnp.exp(sink - m)
        store((_dot(p_loc.astype(BF16), vw) + _dot(p_ctx.astype(BF16), vc)) / den)


def _attn(q, k, v, sink, n_ctx):
    bsz, _, n, _ = q.shape
    n_lat = n - n_ctx
    assert n_lat >= 3 * BLOCK and n_ctx % BLOCK == 0
    return pl.pallas_call(
        functools.partial(_attn_kernel, n_ctx=n_ctx, n_lat=n_lat),
        grid=(bsz, ATT_KV_HEADS, n // BLOCK),
        in_specs=[pl.BlockSpec(memory_space=pltpu.SMEM),
                  pl.BlockSpec((1, ATT_REP, BLOCK, HEAD_DIM), lambda b, h, i: (b, h, i, 0)),
                  pl.BlockSpec((1, 1, n, HEAD_DIM), lambda b, h, i: (b, h, 0, 0)),
                  pl.BlockSpec((1, 1, n, HEAD_DIM), lambda b, h, i: (b, h, 0, 0))],
        out_specs=pl.BlockSpec((1, BLOCK, ATT_REP * HEAD_DIM), lambda b, h, i: (b, i, h)),
        out_shape=jax.ShapeDtypeStruct((bsz, n, ATT_HEADS * HEAD_DIM), F32),
        compiler_params=_cparams(("parallel", "parallel", "parallel"), 32),
        name="window_attn",
    )(sink, q, k, v)


RW_FEATS = 6


def _softplus(y):
    return jnp.maximum(y, 0.0) + jnp.log(1.0 + jnp.exp(-jnp.abs(y)))


def _rw_feat_kernel(x_ref, p_ref, n_ref, mup_ref, mun_ref, w2_ref, a2_ref, w0_ref, a0_ref, kk_ref, ka_ref, rk_ref,
                    j_ref, f_out, v_out, gc_out, bv_out, *, tm, n_ctx, n_tot):
    w = BRANCH_W
    z = x_ref[0]
    rowi = lax.broadcasted_iota(jnp.int32, (tm, 1), 0)
    pos = pl.program_id(1) * tm + rowi
    prev, nxt = _shift_rows(z, p_ref[0, 7:8, :], n_ref[0, 0:1, :],
                            (pos == 0) | (pos == n_ctx), (pos == n_ctx - 1) | (pos == n_tot - 1))
    zz = z + mup_ref[...] * (prev - z) + mun_ref[...] * (nxt - z)
    r, k, v, lo = zz[:, 0:w], zz[:, w:2 * w], zz[:, 2 * w:3 * w], zz[:, 3 * w:]
    seg_j = j_ref[...]
    kk = k * kk_ref[...]
    kk = kk * lax.rsqrt(_segsum(kk * kk, seg_j) + 1e-12)
    w_log = -_softplus(-(_dot(jnp.tanh(lo), w2_ref[...], HI) + w0_ref[...])) - 0.5
    log_decay = -jnp.exp(w_log)
    a = jax.nn.sigmoid(_dot(lo, a2_ref[...], HI) + a0_ref[...])
    ri = lax.broadcasted_iota(jnp.int32, (tm, tm), 0)
    ci = lax.broadcasted_iota(jnp.int32, (tm, tm), 1)
    same = (ri // RW_C) == (ci // RW_C)
    tot_m = jnp.where(same, 1.0, 0.0)
    bonus = jnp.zeros((tm, w), F32)
    for d in range(2):
        ld = log_decay[:, d * w:(d + 1) * w]
        ad = a[:, d * w:(d + 1) * w]
        kd = k * (1.0 + (ad - 1.0) * ka_ref[...])
        b = kk * ad
        bonus = bonus + r * kd * rk_ref[...]
        tri = jnp.where(same & ((ci <= ri) if d == 0 else (ci >= ri)), 1.0, 0.0)
        cum = _dot(tri, ld, HI)
        tot = _dot(tot_m, ld, HI)
        e_neg = jnp.exp(-cum)
        e_hat = jnp.exp(tot - cum)
        feats = (kk * jnp.exp(cum - ld), b * e_neg, kd * e_neg, r * jnp.exp(cum), kd * e_hat, b * e_hat)
        for q in range(RW_FEATS):
            for h in range(RW_HEADS):
                f_out[0, d, q, h] = feats[q][:, h * RW_N:(h + 1) * RW_N]
        g = jnp.exp(tot)
        gc_out[0, d, 0] = jnp.concatenate([g[c * RW_C:c * RW_C + 1] for c in range(tm // RW_C)], axis=0)
    for h in range(RW_HEADS):
        v_out[0, h] = v[:, h * RW_N:(h + 1) * RW_N]
    bv_out[0] = _segsum(bonus, seg_j) * v


def _rw_features(rw, params, seg_j, n_ctx):
    mu_prev, mu_next, w0, w2, a0, a2, k_k, k_a, r_k = params
    bsz, n, wz = rw.shape
    tm = ROW_TILE
    w = BRANCH_W
    nb8 = n // 8
    lw = 4 * RW_LORA
    w2p = jnp.zeros((lw, 2 * w), F32)
    a2p = jnp.zeros((lw, 2 * w), F32)
    for d in range(2):
        w2p = w2p.at[d * RW_LORA:(d + 1) * RW_LORA, d * w:(d + 1) * w].set(w2[d])
        a2p = a2p.at[(2 + d) * RW_LORA:(3 + d) * RW_LORA, d * w:(d + 1) * w].set(a2[d])
    vec = lambda t: t.reshape(1, -1)
    full = lambda shape: pl.BlockSpec(shape, lambda b, j: (0,) * len(shape))
    gpt = tm // RW_C
    return pl.pallas_call(
        functools.partial(_rw_feat_kernel, tm=tm, n_ctx=n_ctx, n_tot=n),
        grid=(bsz, n // tm),
        in_specs=[pl.BlockSpec((1, tm, wz), lambda b, j: (b, j, 0)),
                  pl.BlockSpec((1, 8, wz), lambda b, j: (b, jnp.maximum(j * (tm // 8) - 1, 0), 0)),
                  pl.BlockSpec((1, 8, wz), lambda b, j: (b, jnp.minimum((j + 1) * (tm // 8), nb8 - 1), 0)),
                  full((1, wz)), full((1, wz)), full((lw, 2 * w)), full((lw, 2 * w)),
                  full((1, 2 * w)), full((1, 2 * w)), full((1, w)), full((1, w)), full((1, w)), full((w, w))],
        out_specs=[pl.BlockSpec((1, 2, RW_FEATS, RW_HEADS, tm, RW_N), lambda b, j: (b, 0, 0, 0, j, 0)),
                   pl.BlockSpec((1, RW_HEADS, tm, RW_N), lambda b, j: (b, 0, j, 0)),
                   pl.BlockSpec((1, 2, 1, gpt, w), lambda b, j: (b, 0, j, 0, 0)),
                   pl.BlockSpec((1, tm, w), lambda b, j: (b, j, 0))],
        out_shape=[jax.ShapeDtypeStruct((bsz, 2, RW_FEATS, RW_HEADS, n, RW_N), F32),
                   jax.ShapeDtypeStruct((bsz, RW_HEADS, n, RW_N), F32),
                   jax.ShapeDtypeStruct((bsz, 2, n // tm, gpt, w), F32),
                   jax.ShapeDtypeStruct((bsz, n, w), F32)],
        compiler_params=_cparams(("parallel", "parallel"), 56),
        name="rwkv_features",
    )(rw, rw, rw, vec(mu_prev), vec(mu_next), w2p, a2p, vec(w0), vec(a0), vec(k_k), vec(k_a), vec(r_k), seg_j)


def _tri_inverse(a, ri, ci):
    eye = jnp.where(ri == ci, 1.0, 0.0)
    x = jnp.where((ri // 8) == (ci // 8), -a, 0.0)
    p = eye + x
    x2 = _dot(x, x, HI)
    p = p + _dot(p, x2, HI)
    x4 = _dot(x2, x2, HI)
    p = p + _dot(p, x4, HI)
    blk = 8
    while blk < RW_C:
        e = jnp.where(((ri // (2 * blk)) == (ci // (2 * blk))) & ((ri // blk) != (ci // blk)), a, 0.0)
        p = p - _dot(_dot(p, e, HI), p, HI)
        blk *= 2
    return p


def _rw_chunk_kernel(f_ref, v_ref, gc_ref, mn_out, qo_out):
    c = pl.program_id(1)
    gpt = gc_ref.shape[3]
    ri = lax.broadcasted_iota(jnp.int32, (RW_C, RW_C), 0)
    ci = lax.broadcasted_iota(jnp.int32, (RW_C, RW_C), 1)
    eye = ri == ci
    for d in range(2):
        before = (ci < ri) if d == 0 else (ci > ri)
        upto = before | eye
        g_all = gc_ref[0, d, 0, pl.ds(c % gpt, 1), :]
        for h in range(RW_HEADS):
            kap, bet, kt, rt, kh, bh = (f_ref[0, d, q, h] for q in range(RW_FEATS))
            v = v_ref[0, h]
            a_ab = jnp.where(before, _dot_nt(kap, bet, HI), 0.0)
            a_ak = jnp.where(before, _dot_nt(kap, kt, HI), 0.0)
            a_qb = jnp.where(upto, _dot_nt(rt, bet, HI), 0.0)
            a_qk = jnp.where(upto, _dot_nt(rt, kt, HI), 0.0)
            t_inv = _tri_inverse(a_ab, ri, ci)
            w_m = _dot(t_inv, kap, HI)
            u0 = _dot(t_inv, _dot(a_ak, v, HI), HI)
            g = g_all[:, h * RW_N:(h + 1) * RW_N]
            mn_out[0, d, 0, h, 0] = jnp.where(eye, g, 0.0) - _dot_tn(w_m, bh, HI)
            mn_out[0, d, 1, h, 0] = _dot_tn(v, kh, HI) - _dot_tn(u0, bh, HI)
            qo_out[0, d, 0, h] = rt - _dot(a_qb, w_m, HI)
            qo_out[0, d, 1, h] = _dot(a_qk, v, HI) - _dot(a_qb, u0, HI)


def _rw_chunks(feats, v, gc):
    bsz, _, _, _, n, _ = feats.shape
    nch = n // RW_C
    gpt = gc.shape[3]
    return pl.pallas_call(
        _rw_chunk_kernel,
        grid=(bsz, nch),
        in_specs=[pl.BlockSpec((1, 2, RW_FEATS, RW_HEADS, RW_C, RW_N), lambda b, c: (b, 0, 0, 0, c, 0)),
                  pl.BlockSpec((1, RW_HEADS, RW_C, RW_N), lambda b, c: (b, 0, c, 0)),
                  pl.BlockSpec((1, 2, 1, gpt, BRANCH_W), lambda b, c: (b, 0, c // gpt, 0, 0))],
        out_specs=[pl.BlockSpec((1, 2, 2, RW_HEADS, 1, RW_N, RW_N), lambda b, c: (b, 0, 0, 0, c, 0, 0)),
                   pl.BlockSpec((1, 2, 2, RW_HEADS, RW_C, RW_N), lambda b, c: (b, 0, 0, 0, c, 0))],
        out_shape=[jax.ShapeDtypeStruct((bsz, 2, 2, RW_HEADS, nch, RW_N, RW_N), F32),
                   jax.ShapeDtypeStruct((bsz, 2, 2, RW_HEADS, n, RW_N), F32)],
        compiler_params=_cparams(("parallel", "parallel"), 32),
        name="rwkv_chunks",
    )(feats, v, gc)


def _rw_scan_kernel(mnf_ref, mnb_ref, qof_ref, qob_ref, yf_ref, yb_ref, state):
    @pl.when(pl.program_id(1) == 0)
    def _():
        state[...] = jnp.zeros_like(state)

    for d, (mn_ref, qo_ref, y_ref) in enumerate(((mnf_ref, qof_ref, yf_ref), (mnb_ref, qob_ref, yb_ref))):
        outs = []
        for h in range(RW_HEADS):
            s0 = state[d, h]
            outs.append(_dot_nt(qo_ref[0, 0, 0, h], s0, HI) + qo_ref[0, 0, 1, h])
            state[d, h] = _dot(s0, mn_ref[0, 0, 0, h, 0], HI) + mn_ref[0, 0, 1, h, 0]
        y_ref[0] = jnp.concatenate(outs, axis=1)


def _rw_scan(mn, qo, n_ctx):
    bsz, _, _, _, nch, _, _ = mn.shape
    nch_ctx = n_ctx // RW_C
    n = nch * RW_C

    def rv(s):
        return jnp.where(s < nch_ctx, nch_ctx - 1 - s, nch + nch_ctx - 1 - s)

    mn_spec = lambda d, f: pl.BlockSpec((1, 1, 2, RW_HEADS, 1, RW_N, RW_N), lambda b, s: (b, d, 0, 0, f(s), 0, 0))
    qo_spec = lambda d, f: pl.BlockSpec((1, 1, 2, RW_HEADS, RW_C, RW_N), lambda b, s: (b, d, 0, 0, f(s), 0))
    ident = lambda s: s
    return pl.pallas_call(
        _rw_scan_kernel,
        grid=(bsz, nch),
        in_specs=[mn_spec(0, ident), mn_spec(1, rv), qo_spec(0, ident), qo_spec(1, rv)],
        out_specs=[pl.BlockSpec((1, RW_C, BRANCH_W), lambda b, s: (b, s, 0)),
                   pl.BlockSpec((1, RW_C, BRANCH_W), lambda b, s: (b, rv(s), 0))],
        out_shape=[jax.ShapeDtypeStruct((bsz, n, BRANCH_W), F32), jax.ShapeDtypeStruct((bsz, n, BRANCH_W), F32)],
        scratch_shapes=[pltpu.VMEM((2, RW_HEADS, RW_N, RW_N), F32)],
        compiler_params=_cparams(("parallel", "arbitrary"), 32),
        name="rwkv_scan",
    )(mn, mn, qo, qo)


def _conv3_kernel(x_ref, p_ref, n_ref, w_ref, b_ref, o_ref, *, tm, n_blocks):
    j = pl.program_id(1)
    z = x_ref[0]
    rowi = lax.broadcasted_iota(jnp.int32, (tm, 1), 0)
    prev, nxt = _shift_rows(z, p_ref[0, 7:8, :], n_ref[0, 0:1, :],
                            (rowi == 0) & (j == 0), (rowi == tm - 1) & (j == n_blocks - 1))
    o_ref[0] = prev * w_ref[0:1, :] + z * w_ref[1:2, :] + nxt * w_ref[2:3, :] + b_ref[...]


def _hy_conv3(hy, conv_w, conv_b, seg_start, seg_len):
    bsz, _, w = hy.shape
    tm = ROW_TILE
    off, nb = seg_start // tm, seg_len // tm
    off8, nb8 = seg_start // 8, seg_len // 8
    return pl.pallas_call(
        functools.partial(_conv3_kernel, tm=tm, n_blocks=nb),
        grid=(bsz, nb),
        in_specs=[pl.BlockSpec((1, tm, w), lambda b, j: (b, off + j, 0)),
                  pl.BlockSpec((1, 8, w), lambda b, j: (b, off8 + jnp.maximum(j * (tm // 8) - 1, 0), 0)),
                  pl.BlockSpec((1, 8, w), lambda b, j: (b, off8 + jnp.minimum((j + 1) * (tm // 8), nb8 - 1), 0)),
                  pl.BlockSpec((3, w), lambda b, j: (0, 0)),
                  pl.BlockSpec((1, w), lambda b, j: (0, 0))],
        out_specs=pl.BlockSpec((1, tm, w), lambda b, j: (b, j, 0)),
        out_shape=jax.ShapeDtypeStruct((bsz, seg_len, w), F32),
        compiler_params=_cparams(("parallel", "parallel"), 32),
        name="hyena_conv3",
    )(hy, hy, hy, conv_w, conv_b.reshape(1, w))


def _hy_embedding(n):
    t = jnp.linspace(0.0, 1.0, n, dtype=F32)[:, None]
    ang = 2.0 * math.pi * jnp.arange(n, dtype=F32)[:, None] / n
    bands = jnp.linspace(1e-4, HY_BANDS - 1, HY_BANDS, dtype=F32)[None, :]
    z = jnp.concatenate([t, jnp.cos(bands * ang), -jnp.sin(bands * ang)], axis=-1)
    return jnp.pad(z, ((0, 0), (0, HY_FFN - HY_EMB)))


def _hy_filter_kernel(e_ref, w1_ref, b1_ref, f1_ref, w2_ref, b2_ref, f2_ref, w3_ref, dl_ref, o_ref):
    e = e_ref[...]
    h = jnp.sin(f1_ref[...] * (_dot(e, w1_ref[...], HI) + b1_ref[...]))
    h = jnp.sin(f2_ref[...] * (_dot(h, w2_ref[...], HI) + b2_ref[...]))
    o_ref[...] = _dot(h, w3_ref[...], HI) * jnp.exp(-e[:, 0:1] * dl_ref[...])


def _hy_filters(n, w1, b1, f1, w2, b2, f2, w3):
    tm = min(ROW_TILE, n)
    wf = w3.shape[1]
    emb = _hy_embedding(n)
    w1p = jnp.pad(w1, ((0, HY_FFN - HY_EMB), (0, 0)))
    deltas = jnp.abs(jnp.linspace(math.log(HY_TARGET) / HY_SLOW_PCT, math.log(HY_TARGET) / HY_FAST_PCT,
                                  BRANCH_W, dtype=F32))
    dl = jnp.tile(deltas, wf // BRANCH_W).reshape(1, wf)
    vec = lambda t: t.reshape(1, -1)
    full = lambda shape: pl.BlockSpec(shape, lambda i: (0,) * len(shape))
    return pl.pallas_call(
        _hy_filter_kernel,
        grid=(n // tm,),
        in_specs=[pl.BlockSpec((tm, HY_FFN), lambda i: (i, 0)),
                  full((HY_FFN, HY_FFN)), full((1, HY_FFN)), full((1, HY_FFN)),
                  full((HY_FFN, HY_FFN)), full((1, HY_FFN)), full((1, HY_FFN)),
                  full((HY_FFN, wf)), full((1, wf))],
        out_specs=pl.BlockSpec((tm, wf), lambda i: (i, 0)),
        out_shape=jax.ShapeDtypeStruct((n, wf), F32),
        compiler_params=_cparams(("parallel",), 32),
        name="hyena_filters",
    )(emb, w1p, vec(b1), vec(f1), w2, vec(b2), vec(f2), w3, dl)


def _dft_mats(n):
    k = jnp.arange(n, dtype=jnp.int32)
    ang = (math.pi / n) * ((k[:, None] * k[None, :]) % (2 * n)).astype(F32)
    alt = jnp.where(k % 2 == 0, 1.0, -1.0).astype(F32)
    cos, sin = jnp.cos(ang), jnp.sin(ang)
    fwd = jnp.concatenate([cos, (-sin).at[0].set(alt)], axis=0)
    wk = jnp.where(k == 0, 1.0, 2.0).astype(F32)[None, :]
    inv = jnp.concatenate([wk * cos, (-2.0 * sin).at[:, 0].set(alt)], axis=1) / (2 * n)
    return fwd.astype(BF16), inv.astype(BF16)


def _hy_spec_kernel(fc_ref, fs_ref, h_ref, g_ref, *, tk):
    h = h_ref[...]
    n, w2 = h.shape
    w = w2 // 2
    rowi = lax.broadcasted_iota(jnp.int32, (n, w2), 0)
    coli = lax.broadcasted_iota(jnp.int32, (n, w2), 1)
    h = jnp.where((rowi == 0) & (coli >= w), 0.0, h)
    hi = h.astype(BF16)
    lo = (h - hi.astype(F32)).astype(BF16)
    sr = _dot(fc_ref[...], hi) + _dot(fc_ref[...], lo)
    si = _dot(fs_ref[...], hi) + _dot(fs_ref[...], lo)
    krow = pl.program_id(1) * tk + lax.broadcasted_iota(jnp.int32, (tk, 1), 0)
    g_ref[0, 0] = sr[:, 0:w] + sr[:, w:w2]
    g_ref[0, 1] = si[:, 0:w] + jnp.where(krow == 0, 1.0, -1.0) * si[:, w:w2]


def _hy_spectrum(filt, fwd, n):
    w = BRANCH_W
    tk = min(ROW_TILE, n)
    nk = n // tk
    return pl.pallas_call(
        functools.partial(_hy_spec_kernel, tk=tk),
        grid=(2, nk),
        in_specs=[pl.BlockSpec((tk, n), lambda j, kt: (kt, 0)),
                  pl.BlockSpec((tk, n), lambda j, kt: (nk + kt, 0)),
                  pl.BlockSpec((n, 2 * w), lambda j, kt: (0, j))],
        out_specs=pl.BlockSpec((1, 2, tk, w), lambda j, kt: (j, 0, kt, 0)),
        out_shape=jax.ShapeDtypeStruct((2, 2, n, w), F32),
        compiler_params=_cparams(("parallel", "parallel"), 48),
        name="hyena_filter_spectrum",
    )(fwd, fwd, filt)


def _hy_fwd_kernel(fc_ref, fs_ref, u_ref, g_ref, z_ref):
    u = u_ref[0].astype(BF16)
    ur = _dot(fc_ref[...], u)
    ui = _dot(fs_ref[...], u)
    gr, gi = g_ref[0], g_ref[1]
    tk = ur.shape[0]
    first = (pl.program_id(1) == 0) & (lax.broadcasted_iota(jnp.int32, (tk, 1), 0) == 0)
    z_ref[0, 0] = jnp.where(first, ur * gr, ur * gr - ui * gi).astype(BF16)
    z_ref[0, 1] = jnp.where(first, ui * gi, ur * gi + ui * gr).astype(BF16)


def _hy_inv_kernel(gi_ref, z_ref, u_ref, x_ref, skip_ref, o_ref):
    y = _dot(gi_ref[...], z_ref[0])
    o_ref[0] = x_ref[0] * (y + skip_ref[...] * u_ref[0])


def _hy_long_conv(src, src_col, gate_src, gate_col, spec_j, skip_j, fwd, inv):
    bsz, n, _ = src.shape
    w = BRANCH_W
    tk = min(ROW_TILE, n)
    nk = n // tk
    z = pl.pallas_call(
        _hy_fwd_kernel,
        grid=(bsz, nk),
        in_specs=[pl.BlockSpec((tk, n), lambda b, kt: (kt, 0)),
                  pl.BlockSpec((tk, n), lambda b, kt: (nk + kt, 0)),
                  pl.BlockSpec((1, n, w), lambda b, kt: (b, 0, src_col)),
                  pl.BlockSpec((2, tk, w), lambda b, kt: (0, kt, 0))],
        out_specs=pl.BlockSpec((1, 2, tk, w), lambda b, kt: (b, 0, kt, 0)),
        out_shape=jax.ShapeDtypeStruct((bsz, 2, n, w), BF16),
        compiler_params=_cparams(("parallel", "parallel"), 48),
        name="hyena_dft",
    )(fwd, fwd, src, spec_j)
    z = z.reshape(bsz, 2 * n, w)
    return pl.pallas_call(
        _hy_inv_kernel,
        grid=(bsz, nk),
        in_specs=[pl.BlockSpec((tk, 2 * n), lambda b, t: (t, 0)),
                  pl.BlockSpec((1, 2 * n, w), lambda b, t: (b, 0, 0)),
                  pl.BlockSpec((1, tk, w), lambda b, t: (b, t, src_col)),
                  pl.BlockSpec((1, tk, w), lambda b, t: (b, t, gate_col)),
                  pl.BlockSpec((1, w), lambda b, t: (0, 0))],
        out_specs=pl.BlockSpec((1, tk, w), lambda b, t: (b, t, 0)),
        out_shape=jax.ShapeDtypeStruct((bsz, n, w), F32),
        compiler_params=_cparams(("parallel", "parallel"), 48),
        name="hyena_idft",
    )(inv, z, src, gate_src, skip_j.reshape(1, w))


def _hyena_segment(hy, seg_start, seg_len, conv_w, conv_b, ffn, skip):
    z = _hy_conv3(hy, conv_w, conv_b, seg_start, seg_len)
    filt = _hy_filters(seg_len, *ffn)
    fwd, inv = _dft_mats(seg_len)
    spec = _hy_spectrum(filt, fwd, seg_len)
    y1 = _hy_long_conv(z, 0, z, 1, spec[0], skip[0], fwd, inv)
    return _hy_long_conv(y1, 0, z, 2, spec[1], skip[1], fwd, inv)


def _merge_kernel(x_ref, gx_ref, gc_ref, s5u_ref, s5f_ref, s5b_ref, d_ref, gw_ref, gb_ref, att_ref,
                  rwf_ref, rwb_ref, bv_ref, lng_ref, lnb_ref, hyl_ref, hyc_ref, gate_ref, bg_ref, wo_ref, j_ref,
                  o_ref, *, n_ctx_blocks, blk_off):
    w = BRANCH_W
    is_ctx = (pl.program_id(1) + blk_off) < n_ctx_blocks

    def rms(y, g):
        return y * lax.rsqrt(jnp.mean(y * y, axis=-1, keepdims=True) + NORM_EPS) * g

    ys = d_ref[...] * s5u_ref[0] + s5f_ref[0] + s5b_ref[0]
    ys = 0.5 * ys * (1.0 + lax.erf(ys * (2.0 ** -0.5)))
    ys = ys * jax.nn.sigmoid(_dot(ys.astype(BF16), gw_ref[...]) + gb_ref[...])
    ys = rms(ys, bg_ref[0:1, :])
    ya = rms(att_ref[0], bg_ref[1:2, :])
    seg_j = j_ref[...]
    yr = rwf_ref[0] + rwb_ref[0]
    mu = _segsum(yr, seg_j) * (1.0 / RW_N)
    yc = yr - mu
    var = _segsum(yc * yc, seg_j) * (1.0 / RW_N)
    yr = yc * lax.rsqrt(var + RW_LN_EPS) * lng_ref[...] + lnb_ref[...] + bv_ref[0]
    yh = rms(jnp.where(is_ctx, hyc_ref[0], hyl_ref[0]), bg_ref[2:3, :])
    gp = gate_ref[0]
    sg = gp * jax.nn.sigmoid(gp)
    acc = _dot((ys * sg[:, 0:w]).astype(BF16), wo_ref[0:w, :])
    acc += _dot((ya * sg[:, w:2 * w]).astype(BF16), wo_ref[w:2 * w, :])
    acc += _dot((yr * sg[:, 2 * w:3 * w]).astype(BF16), wo_ref[2 * w:3 * w, :])
    acc += _dot((yh * sg[:, 3 * w:4 * w]).astype(BF16), wo_ref[3 * w:4 * w, :])
    gate = jnp.where(is_ctx, gc_ref[0, 0], gx_ref[0, 0])
    o_ref[0] = x_ref[0] + gate * acc


def _merge(x_all, mod_l, s5u, s5f, s5b, d_skip, glu_w, glu_b, att, rwf, rwb, bv, ln_g, ln_b, hy_lat, hy_ctx,
           gate_pre, branch_g, w_out, seg_j, n_ctx, with_ctx):
    bsz, n, d = x_all.shape
    w = BRANCH_W
    tm = ROW_TILE
    ncb = n_ctx // tm
    off = 0 if with_ctx else ncb
    nb = n // tm - off
    nlb = (n - n_ctx) // tm
    ctx_row = bsz
    row = lambda width: pl.BlockSpec((1, tm, width), lambda b, j: (b, j + off, 0))
    full = lambda shape: pl.BlockSpec(shape, lambda b, j: (0,) * len(shape))
    if hy_ctx is None:
        hy_ctx = hy_lat
        hyc_spec = pl.BlockSpec((1, tm, w), lambda b, j: (b, 0, 0))
    else:
        hyc_spec = pl.BlockSpec((1, tm, w), lambda b, j: (b, jnp.minimum(j + off, ncb - 1), 0))
    vec = lambda t: t.reshape(1, -1)
    return pl.pallas_call(
        functools.partial(_merge_kernel, n_ctx_blocks=ncb, blk_off=off),
        grid=(bsz, nb),
        in_specs=[row(d),
                  pl.BlockSpec((1, 1, 1, d), lambda b, j: (b, 2, 0, 0)),
                  pl.BlockSpec((1, 1, 1, d), lambda b, j: (ctx_row, 2, 0, 0)),
                  row(w), row(w), row(w), full((1, w)), full((w, w)), full((1, w)),
                  row(w), row(w), row(w), row(w), full((1, w)), full((1, w)),
                  pl.BlockSpec((1, tm, w), lambda b, j: (b, jnp.clip(j + off - ncb, 0, nlb - 1), 0)),
                  hyc_spec, row(4 * w), full((3, w)), full((4 * w, d)), full((w, w))],
        out_specs=pl.BlockSpec((1, tm, d), lambda b, j: (b, j, 0)),
        out_shape=jax.ShapeDtypeStruct((bsz, nb * tm, d), F32),
        compiler_params=_cparams(("parallel", "parallel"), 56),
        name="merge_out_proj",
    )(x_all, mod_l, mod_l, s5u, s5f, s5b, vec(d_skip), glu_w.astype(BF16), vec(glu_b), att,
      rwf, rwb, bv, vec(ln_g), vec(ln_b), hy_lat, hy_ctx, gate_pre, branch_g, w_out, seg_j)


def kernel(x, c, ctx, c_ctx, norm_g, w_ada, b_ada, w_in, w_out, branch_g, s5_lam_re, s5_lam_im, s5_log_step, s5_b_re, s5_b_im, s5_c_re, s5_c_im, s5_d, s5_glu_w, s5_glu_b, att_q_g, att_k_g, att_sink, rw_mu_prev, rw_mu_next, rw_w0, rw_w2, rw_a0, rw_a2, rw_k_k, rw_k_a, rw_r_k, rw_ln_g, rw_ln_b, hy_conv_w, hy_conv_b, hy_w1, hy_b1, hy_f1, hy_w2, hy_b2, hy_f2, hy_w3, hy_skip):
    bsz, n_lat, d = x.shape
    n_ctx = ctx.shape[1]
    depth = w_ada.shape[0]
    n = n_ctx + n_lat
    assert bsz + 1 <= MOD_ROWS and n_ctx % ROW_TILE == 0 and n_lat % ROW_TILE == 0

    c_all = jnp.zeros((MOD_ROWS, d), F32).at[0:bsz].set(c).at[bsz].set(c_ctx)
    mod = _ada_mod(c_all, w_ada, b_ada).reshape(depth, MOD_ROWS, 3, 1, d)
    w_in_b = w_in.astype(BF16)
    w_out_b = w_out.astype(BF16)
    rope_cos, rope_sin = _rope_tables(n_ctx, n_lat)
    lane = jnp.arange(BRANCH_W)
    seg_j = (lane[:, None] // HEAD_DIM == lane[None, :] // HEAD_DIM).astype(BF16)
    n_scan_steps = max(1, math.ceil(math.log2(n // S5_T)))
    slabs = (P_S5, P_QKV, P_RW, P_HY, P_GATE)
    tiles = (P_S5, P_QKV, P_RW, P_HY // 2, P_GATE // 2)
    offs = [sum(slabs[:i]) for i in range(len(slabs))]

    x_all = jnp.concatenate([ctx, x], axis=1)
    for l in range(depth):
        with_ctx = l < depth - 1
        h = _norm_mod(x_all, norm_g[l], mod[l], n_ctx).reshape(bsz * n, d)
        s5u, qkv, rw, hy, gate_pre = (
            _mm(h, w_in_b[l][:, o:o + wd], 512, tn).reshape(bsz, n, wd) for o, wd, tn in zip(offs, slabs, tiles))

        s5_w = _s5_weights(s5_lam_re[l], s5_lam_im[l], s5_log_step[l], s5_b_re[l], s5_b_im[l],
                           s5_c_re[l], s5_c_im[l], n_scan_steps)
        s5f, s5b = _s5_mix(s5u, s5_w, n_ctx)

        qh, kh, vh = _qk_prep(qkv, rope_cos, rope_sin, att_q_g[l], att_k_g[l], seg_j)
        att = _attn(qh, kh, vh, att_sink[l], n_ctx)

        feats, rv, gc, bv = _rw_features(
            rw, (rw_mu_prev[l], rw_mu_next[l], rw_w0[l], rw_w2[l], rw_a0[l], rw_a2[l], rw_k_k[l], rw_k_a[l],
                 rw_r_k[l].reshape(-1)), seg_j, n_ctx)
        mn, qo = _rw_chunks(feats, rv, gc)
        rwf, rwb = _rw_scan(mn, qo, n_ctx)

        ffn = (hy_w1[l], hy_b1[l], hy_f1[l], hy_w2[l], hy_b2[l], hy_f2[l], hy_w3[l])
        hy_lat = _hyena_segment(hy, n_ctx, n_lat, hy_conv_w[l], hy_conv_b[l], ffn, hy_skip[l])
        hy_ctx = _hyena_segment(hy, 0, n_ctx, hy_conv_w[l], hy_conv_b[l], ffn, hy_skip[l]) if with_ctx else None

        x_all = _merge(x_all, mod[l], s5u, s5f, s5b, s5_d[l].reshape(-1), s5_glu_w[l], s5_glu_b[l], att,
                       rwf, rwb, bv, rw_ln_g[l], rw_ln_b[l], hy_lat, hy_ctx, gate_pre, branch_g[l], w_out_b[l],
                       seg_j, n_ctx, with_ctx)
    return x_all
```

```python
import functools
import math

import jax
import jax.numpy as jnp
from jax import lax
from jax.experimental import pallas as pl
from jax.experimental.pallas import tpu as pltpu

F32 = jnp.float32
BF16 = jnp.bfloat16
HI = lax.Precision.HIGHEST

D_MODEL = 2048
GRID_W = 64
BRANCH_W = 512
NORM_EPS = 1e-6
S5_H = 16
S5_G = BRANCH_W // S5_H
S5_P = 64
HEAD_DIM = 64
ATT_HEADS = 8
ATT_KV_HEADS = 2
ATT_REP = ATT_HEADS // ATT_KV_HEADS
WINDOW = 128
BLOCK = 128
ROPE_BASE = 10000.0
NEG_INF = -1e30
RW_N = 64
RW_HEADS = 8
RW_LORA = 32
RW_LN_EPS = 64e-5
HY_EMB = 33
HY_BANDS = 16
HY_FFN = 64
HY_TARGET = 1e-2
HY_FAST_PCT = 0.3
HY_SLOW_PCT = 1.5
P_S5 = BRANCH_W
P_QKV = ATT_HEADS * HEAD_DIM + 2 * ATT_KV_HEADS * HEAD_DIM
P_RW = 3 * BRANCH_W + 4 * RW_LORA
P_HY = 3 * BRANCH_W
P_GATE = 4 * BRANCH_W

V7X_VMEM_BYTES = 64 * 1024 * 1024
S5_T = 16
RW_C = 64
ROW_TILE = 256
MOD_ROWS = 24


def _cparams(sem, vmem_mb):
    assert vmem_mb * 1024 * 1024 <= V7X_VMEM_BYTES
    return pltpu.CompilerParams(dimension_semantics=sem, vmem_limit_bytes=vmem_mb * 1024 * 1024)


def _dot(a, b, precision=None):
    return jnp.dot(a, b, precision=precision, preferred_element_type=F32)


def _dot_nt(a, b, precision=None):
    return lax.dot_general(a, b, (((1,), (1,)), ((), ())), precision=precision, preferred_element_type=F32)


def _dot_tn(a, b, precision=None):
    return lax.dot_general(a, b, (((0,), (0,)), ((), ())), precision=precision, preferred_element_type=F32)


NN = ((1,), (0,))
NT = ((1,), (1,))
TN = ((0,), (0,))


def _split(x):
    hi = x.astype(BF16)
    return hi, (x - hi.astype(F32)).astype(BF16)


def _mx(a, b, dims, passes):
    dg = lambda p, q: lax.dot_general(p, q, (dims, ((), ())), preferred_element_type=F32)
    if passes == 1:
        return dg(a.astype(BF16), b.astype(BF16))
    ah, al = _split(a)
    bh, bl = _split(b)
    return dg(ah, bh) + dg(ah, bl) + dg(al, bh)


def _mx_exact_lhs(a_bf16, b):
    bh, bl = _split(b)
    return _dot(a_bf16, bh) + _dot(a_bf16, bl)


def _segsum(x, j):
    hi, lo = _split(x)
    return _dot(hi, j) + _dot(lo, j)


def _shift_rows(z, prev_row, next_row, zero_prev, zero_next):
    tm = z.shape[0]
    rowi = lax.broadcasted_iota(jnp.int32, (tm, 1), 0)
    prev = jnp.where(rowi == 0, prev_row, pltpu.roll(z, 1, 0))
    nxt = jnp.where(rowi == tm - 1, next_row, pltpu.roll(z, tm - 1, 0))
    prev = jnp.where(zero_prev, 0.0, prev)
    nxt = jnp.where(zero_next, 0.0, nxt)
    return prev, nxt


def _mod_kernel(c_ref, w_ref, b_ref, o_ref):
    c = c_ref[...]
    s = c * jax.nn.sigmoid(c)
    o_ref[0] = _dot(s, w_ref[0], HI) + b_ref[0]


def _ada_mod(c_all, w_ada, b_ada):
    depth, d, d3 = w_ada.shape
    tn = 512
    return pl.pallas_call(
        _mod_kernel,
        grid=(depth, d3 // tn),
        in_specs=[pl.BlockSpec((MOD_ROWS, d), lambda l, j: (0, 0)),
                  pl.BlockSpec((1, d, tn), lambda l, j: (l, 0, j)),
                  pl.BlockSpec((1, 1, tn), lambda l, j: (l, 0, j))],
        out_specs=pl.BlockSpec((1, MOD_ROWS, tn), lambda l, j: (l, 0, j)),
        out_shape=jax.ShapeDtypeStruct((depth, MOD_ROWS, d3), F32),
        compiler_params=_cparams(("parallel", "parallel"), 32),
        name="ada_mod",
    )(c_all, w_ada, b_ada.reshape(depth, 1, d3))


def _norm_mod_kernel(x_ref, g_ref, shx_ref, scx_ref, shc_ref, scc_ref, o_ref, *, n_ctx_blocks):
    x = x_ref[0]
    ms = jnp.mean(x * x, axis=-1, keepdims=True)
    y = x * lax.rsqrt(ms + NORM_EPS) * g_ref[...]
    is_ctx = pl.program_id(1) < n_ctx_blocks
    sc = jnp.where(is_ctx, scc_ref[0, 0], scx_ref[0, 0])
    sh = jnp.where(is_ctx, shc_ref[0, 0], shx_ref[0, 0])
    o_ref[0] = (y * (1.0 + sc) + sh).astype(BF16)


def _norm_mod(x_all, g, mod_l, n_ctx):
    bsz, n, d = x_all.shape
    tm = ROW_TILE
    ctx_row = bsz
    return pl.pallas_call(
        functools.partial(_norm_mod_kernel, n_ctx_blocks=n_ctx // tm),
        grid=(bsz, n // tm),
        in_specs=[pl.BlockSpec((1, tm, d), lambda b, j: (b, j, 0)),
                  pl.BlockSpec((1, d), lambda b, j: (0, 0)),
                  pl.BlockSpec((1, 1, 1, d), lambda b, j: (b, 0, 0, 0)),
                  pl.BlockSpec((1, 1, 1, d), lambda b, j: (b, 1, 0, 0)),
                  pl.BlockSpec((1, 1, 1, d), lambda b, j: (ctx_row, 0, 0, 0)),
                  pl.BlockSpec((1, 1, 1, d), lambda b, j: (ctx_row, 1, 0, 0))],
        out_specs=pl.BlockSpec((1, tm, d), lambda b, j: (b, j, 0)),
        out_shape=jax.ShapeDtypeStruct((bsz, n, d), BF16),
        compiler_params=_cparams(("parallel", "parallel"), 32),
        name="norm_mod",
    )(x_all, g.reshape(1, d), mod_l, mod_l, mod_l, mod_l)


def _mm_kernel(a_ref, b_ref, o_ref):
    o_ref[...] = _dot(a_ref[...], b_ref[...]).astype(o_ref.dtype)


def _mm(a, b, tm, tn):
    m, k = a.shape
    n = b.shape[1]
    assert m % tm == 0 and n % tn == 0
    return pl.pallas_call(
        _mm_kernel,
        grid=(n // tn, m // tm),
        in_specs=[pl.BlockSpec((tm, k), lambda j, i: (i, 0)),
                  pl.BlockSpec((k, tn), lambda j, i: (0, j))],
        out_specs=pl.BlockSpec((tm, tn), lambda j, i: (i, j)),
        out_shape=jax.ShapeDtypeStruct((m, n), F32),
        compiler_params=_cparams(("parallel", "parallel"), 48),
        name="in_proj",
    )(a, b)


def _s5_weights(lam_re, lam_im, log_step, b_re, b_im, c_re, c_im, n_scan_steps, nc_lat, bsz):
    t_len = S5_T
    step = jnp.exp(log_step)[..., None]
    th_re, th_im = lam_re * step, lam_im * step

    def cpow(k):
        mag = jnp.exp(th_re[..., None] * k)
        ang = th_im[..., None] * k
        return mag * jnp.cos(ang), mag * jnp.sin(ang)

    lb_re, lb_im = (t[..., 0] for t in cpow(jnp.ones((1,), F32)))
    den = lam_re * lam_re + lam_im * lam_im
    nr = lb_re - 1.0
    co_re = (nr * lam_re + lb_im * lam_im) / den
    co_im = (lb_im * lam_re - nr * lam_im) / den
    bb_re = co_re[..., None] * b_re - co_im[..., None] * b_im
    bb_im = co_re[..., None] * b_im + co_im[..., None] * b_re

    lags = jnp.arange(t_len + 1, dtype=F32)
    pw_re, pw_im = cpow(lags)
    x_re = c_re[..., None] * pw_re[:, :, None] - c_im[..., None] * pw_im[:, :, None]
    x_im = c_re[..., None] * pw_im[:, :, None] + c_im[..., None] * pw_re[:, :, None]
    m_k = (jnp.einsum('dgopk,dgpi->dgkio', x_re, bb_re, precision=HI)
           - jnp.einsum('dgopk,dgpi->dgkio', x_im, bb_im, precision=HI))
    s_idx = jnp.arange(t_len)[:, None]
    t_idx = jnp.arange(t_len)[None, :]
    lag_f = t_idx - s_idx
    kin = []
    for d, lag in enumerate((lag_f, -lag_f)):
        blk = m_k[d][:, jnp.clip(lag, 0, t_len)]
        blk = jnp.where((lag >= 0)[None, :, :, None, None], blk, 0.0)
        kin.append(blk.transpose(0, 1, 3, 2, 4).reshape(S5_G, t_len * S5_H, t_len * S5_H))
    kin = jnp.stack(kin)

    tt = jnp.arange(t_len)
    win, wout = [], []
    for d in range(2):
        e_in = (t_len - 1 - tt) if d == 0 else tt
        e_out = (tt + 1) if d == 0 else (t_len - tt)
        pr, pi = pw_re[d][..., e_in], pw_im[d][..., e_in]
        wr = pr[..., None] * bb_re[d][:, :, None] - pi[..., None] * bb_im[d][:, :, None]
        wi = pr[..., None] * bb_im[d][:, :, None] + pi[..., None] * bb_re[d][:, :, None]
        w = jnp.concatenate([wr, wi], axis=1)
        win.append(w.transpose(0, 2, 3, 1).reshape(S5_G, t_len * S5_H, 2 * S5_P))
        qr, qi = pw_re[d][..., e_out], pw_im[d][..., e_out]
        orr = c_re[d].transpose(0, 2, 1)[:, :, None] * qr[..., None] - c_im[d].transpose(0, 2, 1)[:, :, None] * qi[..., None]
        oii = -(c_re[d].transpose(0, 2, 1)[:, :, None] * qi[..., None] + c_im[d].transpose(0, 2, 1)[:, :, None] * qr[..., None])
        o = jnp.concatenate([orr, oii], axis=1)
        wout.append(o.reshape(S5_G, 2 * S5_P, t_len * S5_H))
    win, wout = jnp.stack(win), jnp.stack(wout)

    sc_re, sc_im = cpow(t_len * (2.0 ** jnp.arange(n_scan_steps, dtype=F32)))
    a1 = jnp.concatenate([sc_re, sc_re], axis=2)
    a2 = jnp.concatenate([-sc_im, sc_im], axis=2)
    pw = jnp.stack([a1, a2], axis=-1).transpose(0, 1, 3, 4, 2).reshape(2, S5_G, 2 * n_scan_steps, 2 * S5_P)
    cr, ci_ = (t[1] for t in cpow(t_len * (nc_lat - 1 - jnp.arange(nc_lat, dtype=F32))))
    tab = jnp.stack([jnp.concatenate([cr, cr], axis=1), jnp.concatenate([-ci_, ci_], axis=1)], axis=1)
    tab = jnp.repeat(tab.transpose(0, 1, 3, 2), bsz, axis=2)
    return kin.astype(BF16), win.astype(BF16), wout.astype(BF16), pw, tab


def _s5_kernel(x_ref, kin_ref, win_ref, wout_ref, pw_ref, tab_ref, y_ref, sa, sb, *, rows, rows_ctx, bsz, n_steps):
    x = x_ref[0]

    def cmul(a1, a2, h):
        return a1 * h + a2 * pltpu.roll(h, S5_P, 1)

    sa[...] = _dot(x, win_ref[0, 0])
    src, dst = sa, sb
    for i in range(n_steps):
        sh = (1 << i) * bsz
        a1 = pw_ref[0, 0, 2 * i:2 * i + 1, :]
        a2 = pw_ref[0, 0, 2 * i + 1:2 * i + 2, :]
        dst[0:sh, :] = src[0:sh, :]
        dst[sh:rows, :] = src[sh:rows, :] + cmul(a1, a2, src[0:rows - sh, :])
        src, dst = dst, src
    y_ref[0] = _dot(x, kin_ref[0, 0]) + _dot(x, kin_ref[1, 0])
    y_ref[0, bsz:rows, :] += _dot(src[0:rows - bsz, :].astype(BF16), wout_ref[0, 0])

    sa[...] = _dot(x, win_ref[1, 0])
    src, dst = sa, sb
    for i in range(n_steps):
        sh = (1 << i) * bsz
        a1 = pw_ref[1, 0, 2 * i:2 * i + 1, :]
        a2 = pw_ref[1, 0, 2 * i + 1:2 * i + 2, :]
        for lo, hi in ((0, rows_ctx), (rows_ctx, rows)):
            if sh < hi - lo:
                dst[lo:hi - sh, :] = src[lo:hi - sh, :] + cmul(a1, a2, src[lo + sh:hi, :])
                dst[hi - sh:hi, :] = src[hi - sh:hi, :]
            else:
                dst[lo:hi, :] = src[lo:hi, :]
        src, dst = dst, src
    rows_lat = rows - rows_ctx
    h_ctx = src[0:bsz, :]
    h_ctx = jnp.broadcast_to(h_ctx[None], (rows_lat // bsz, bsz, 2 * S5_P)).reshape(rows_lat, 2 * S5_P)
    cross = cmul(tab_ref[0, 0], tab_ref[0, 1], h_ctx)
    dst[0:rows_ctx - bsz, :] = src[bsz:rows_ctx, :]
    dst[rows_ctx - bsz:rows_ctx, :] = jnp.zeros((bsz, 2 * S5_P), F32)
    dst[rows_ctx:rows - bsz, :] = src[rows_ctx + bsz:rows, :] + cross[0:rows_lat - bsz]
    dst[rows - bsz:rows, :] = cross[rows_lat - bsz:]
    y_ref[0] += _dot(dst[...].astype(BF16), wout_ref[1, 0])


def _s5_mix(u, weights, n_ctx):
    kin, win, wout, pw, tab = weights
    bsz, n, _ = u.shape
    nc, nc_ctx = n // S5_T, n_ctx // S5_T
    th = S5_T * S5_H
    rows, rows_ctx = nc * bsz, nc_ctx * bsz
    n_steps = pw.shape[2] // 2
    assert (1 << n_steps) >= nc and bsz % 8 == 0 and nc_ctx >= 1
    x = u.reshape(bsz, nc, S5_T, S5_G, S5_H).transpose(3, 1, 0, 2, 4).reshape(S5_G, rows, th).astype(BF16)
    y = pl.pallas_call(
        functools.partial(_s5_kernel, rows=rows, rows_ctx=rows_ctx, bsz=bsz, n_steps=n_steps),
        grid=(S5_G,),
        in_specs=[pl.BlockSpec((1, rows, th), lambda g: (g, 0, 0)),
                  pl.BlockSpec((2, 1, th, th), lambda g: (0, g, 0, 0)),
                  pl.BlockSpec((2, 1, th, 2 * S5_P), lambda g: (0, g, 0, 0)),
                  pl.BlockSpec((2, 1, 2 * S5_P, th), lambda g: (0, g, 0, 0)),
                  pl.BlockSpec((2, 1, 2 * n_steps, 2 * S5_P), lambda g: (0, g, 0, 0)),
                  pl.BlockSpec((1, 2, rows - rows_ctx, 2 * S5_P), lambda g: (g, 0, 0, 0))],
        out_specs=pl.BlockSpec((1, rows, th), lambda g: (g, 0, 0)),
        out_shape=jax.ShapeDtypeStruct((S5_G, rows, th), F32),
        scratch_shapes=[pltpu.VMEM((rows, 2 * S5_P), F32), pltpu.VMEM((rows, 2 * S5_P), F32)],
        compiler_params=_cparams(("parallel",), 32),
        name="s5_chunks",
    )(x, kin, win, wout, pw, tab)
    return y.reshape(S5_G, nc, bsz, S5_T, S5_H).transpose(2, 1, 3, 0, 4).reshape(bsz, n, BRANCH_W)


def _rope_tables(n_ctx, n_lat):
    quarter = HEAD_DIM // 4
    inv = 1.0 / (ROPE_BASE ** (jnp.arange(quarter, dtype=F32) / quarter))
    t = jnp.arange(n_lat, dtype=jnp.int32)
    ar = (t // GRID_W).astype(F32)[:, None] * inv[None, :]
    ac = (t % GRID_W).astype(F32)[:, None] * inv[None, :]
    cos = jnp.concatenate([jnp.cos(ar), jnp.cos(ar), jnp.cos(ac), jnp.cos(ac)], axis=1)
    sin = jnp.concatenate([-jnp.sin(ar), jnp.sin(ar), -jnp.sin(ac), jnp.sin(ac)], axis=1)
    cos = jnp.concatenate([jnp.ones((n_ctx, HEAD_DIM), F32), cos], axis=0)
    sin = jnp.concatenate([jnp.zeros((n_ctx, HEAD_DIM), F32), sin], axis=0)
    return jnp.concatenate([cos, cos], axis=1), jnp.concatenate([sin, sin], axis=1)


def _qk_prep_kernel(x_ref, cos_ref, sin_ref, qg_ref, kg_ref, j_ref, q_out, k_out, v_out):
    x = x_ref[0]
    wq, wk = ATT_HEADS * HEAD_DIM, ATT_KV_HEADS * HEAD_DIM
    cos, sin = cos_ref[...], sin_ref[...]

    def prep(t, g, w, scale):
        ss = _segsum(t * t, j_ref[0:w, 0:w])
        tn = t * lax.rsqrt(ss * (1.0 / HEAD_DIM) + NORM_EPS) * g
        reps = w // 128
        c = jnp.concatenate([cos] * reps, axis=1) if reps > 1 else cos
        s = jnp.concatenate([sin] * reps, axis=1) if reps > 1 else sin
        lane = lax.broadcasted_iota(jnp.int32, tn.shape, 1)
        first = (lane % 32) < 16
        swapped = jnp.where(first, pltpu.roll(tn, w - 16, 1), pltpu.roll(tn, 16, 1))
        return ((tn * c + swapped * s) * scale).astype(BF16)

    q = prep(x[:, 0:wq], qg_ref[...], wq, HEAD_DIM ** -0.5)
    k = prep(x[:, wq:wq + wk], kg_ref[...], wk, 1.0)
    v = x[:, wq + wk:wq + 2 * wk].astype(BF16)
    for h in range(ATT_HEADS):
        q_out[0, h] = q[:, h * HEAD_DIM:(h + 1) * HEAD_DIM]
    for h in range(ATT_KV_HEADS):
        k_out[0, h] = k[:, h * HEAD_DIM:(h + 1) * HEAD_DIM]
        v_out[0, h] = v[:, h * HEAD_DIM:(h + 1) * HEAD_DIM]


def _qk_prep(qkv, cos, sin, q_g, k_g, seg_j):
    bsz, n, w = qkv.shape
    tm = ROW_TILE
    wq, wk = ATT_HEADS * HEAD_DIM, ATT_KV_HEADS * HEAD_DIM
    return pl.pallas_call(
        _qk_prep_kernel,
        grid=(bsz, n // tm),
        in_specs=[pl.BlockSpec((1, tm, w), lambda b, j: (b, j, 0)),
                  pl.BlockSpec((tm, 128), lambda b, j: (j, 0)),
                  pl.BlockSpec((tm, 128), lambda b, j: (j, 0)),
                  pl.BlockSpec((1, wq), lambda b, j: (0, 0)),
                  pl.BlockSpec((1, wk), lambda b, j: (0, 0)),
                  pl.BlockSpec((BRANCH_W, BRANCH_W), lambda b, j: (0, 0))],
        out_specs=[pl.BlockSpec((1, ATT_HEADS, tm, HEAD_DIM), lambda b, j: (b, 0, j, 0)),
                   pl.BlockSpec((1, ATT_KV_HEADS, tm, HEAD_DIM), lambda b, j: (b, 0, j, 0)),
                   pl.BlockSpec((1, ATT_KV_HEADS, tm, HEAD_DIM), lambda b, j: (b, 0, j, 0))],
        out_shape=[jax.ShapeDtypeStruct((bsz, ATT_HEADS, n, HEAD_DIM), BF16),
                   jax.ShapeDtypeStruct((bsz, ATT_KV_HEADS, n, HEAD_DIM), BF16),
                   jax.ShapeDtypeStruct((bsz, ATT_KV_HEADS, n, HEAD_DIM), BF16)],
        compiler_params=_cparams(("parallel", "parallel"), 32),
        name="qk_prep",
    )(qkv, cos, sin, jnp.tile(q_g, ATT_HEADS).reshape(1, wq), jnp.tile(k_g, ATT_KV_HEADS).reshape(1, wk), seg_j)


def _attn_kernel(sink_ref, q_ref, k_ref, v_ref, o_ref, *, n_ctx, n_lat):
    kvh = pl.program_id(1)
    i = pl.program_id(2)
    n_ctx_blocks = n_ctx // BLOCK
    rows = ATT_REP * BLOCK
    span = 3 * BLOCK
    q = q_ref[0].reshape(rows, HEAD_DIM)
    kc = k_ref[0, 0, 0:n_ctx, :]
    vc = v_ref[0, 0, 0:n_ctx, :]
    s_ctx = _dot_nt(q, kc)
    rowi = lax.broadcasted_iota(jnp.int32, (rows, 1), 0)
    rep = rowi // BLOCK
    sink = jnp.zeros((rows, 1), F32)
    for r in range(ATT_REP):
        sink = jnp.where(rep == r, sink_ref[kvh * ATT_REP + r], sink)

    def store(o):
        o_ref[0] = jnp.concatenate([o[r * BLOCK:(r + 1) * BLOCK] for r in range(ATT_REP)], axis=1)

    @pl.when(i < n_ctx_blocks)
    def _():
        m = jnp.maximum(jnp.max(s_ctx, axis=1, keepdims=True), sink)
        p = jnp.exp(s_ctx - m)
        den = jnp.sum(p, axis=1, keepdims=True) + jnp.exp(sink - m)
        store(_dot(p.astype(BF16), vc) / den)

    @pl.when(i >= n_ctx_blocks)
    def _():
        il = i - n_ctx_blocks
        start = pl.multiple_of(jnp.clip((il - 1) * BLOCK, 0, n_lat - span), BLOCK)
        kw = k_ref[0, 0, pl.ds(n_ctx + start, span), :]
        vw = v_ref[0, 0, pl.ds(n_ctx + start, span), :]
        s_loc = _dot_nt(q, kw)
        qpos = il * BLOCK + rowi % BLOCK
        kpos = start + lax.broadcasted_iota(jnp.int32, (rows, span), 1)
        s_loc = jnp.where(jnp.abs(qpos - kpos) <= WINDOW, s_loc, NEG_INF)
        m = jnp.maximum(jnp.maximum(jnp.max(s_loc, axis=1, keepdims=True), jnp.max(s_ctx, axis=1, keepdims=True)), sink)
        p_loc = jnp.exp(s_loc - m)
        p_ctx = jnp.exp(s_ctx - m)
        den = jnp.sum(p_loc, axis=1, keepdims=True) + jnp.sum(p_ctx, axis=1, keepdims=True) + jnp.exp(sink - m)
        store((_dot(p_loc.astype(BF16), vw) + _dot(p_ctx.astype(BF16), vc)) / den)


def _attn(q, k, v, sink, n_ctx):
    bsz, _, n, _ = q.shape
    n_lat = n - n_ctx
    assert n_lat >= 3 * BLOCK and n_ctx % BLOCK == 0
    return pl.pallas_call(
        functools.partial(_attn_kernel, n_ctx=n_ctx, n_lat=n_lat),
        grid=(bsz, ATT_KV_HEADS, n // BLOCK),
        in_specs=[pl.BlockSpec(memory_space=pltpu.SMEM),
                  pl.BlockSpec((1, ATT_REP, BLOCK, HEAD_DIM), lambda b, h, i: (b, h, i, 0)),
                  pl.BlockSpec((1, 1, n, HEAD_DIM), lambda b, h, i: (b, h, 0, 0)),
                  pl.BlockSpec((1, 1, n, HEAD_DIM), lambda b, h, i: (b, h, 0, 0))],
        out_specs=pl.BlockSpec((1, BLOCK, ATT_REP * HEAD_DIM), lambda b, h, i: (b, i, h)),
        out_shape=jax.ShapeDtypeStruct((bsz, n, ATT_HEADS * HEAD_DIM), F32),
        compiler_params=_cparams(("parallel", "parallel", "parallel"), 32),
        name="window_attn",
    )(sink, q, k, v)


RW_FEATS = 6
RW_PASSES_SCORE = 1
RW_PASSES_INV = 1
RW_PASSES_MIX = 1
RW_PASSES_STATE = 3
RW_PASSES_OUT = 1


def _softplus(y):
    return jnp.maximum(y, 0.0) + jnp.log(1.0 + jnp.exp(-jnp.abs(y)))


def _rw_feat_kernel(x_ref, p_ref, n_ref, mup_ref, mun_ref, w2_ref, a2_ref, w0_ref, a0_ref, kk_ref, ka_ref, rk_ref,
                    j_ref, f_out, v_out, gc_out, bv_out, *, tm, n_ctx, n_tot):
    w = BRANCH_W
    z = x_ref[0]
    rowi = lax.broadcasted_iota(jnp.int32, (tm, 1), 0)
    pos = pl.program_id(1) * tm + rowi
    prev, nxt = _shift_rows(z, p_ref[0, 7:8, :], n_ref[0, 0:1, :],
                            (pos == 0) | (pos == n_ctx), (pos == n_ctx - 1) | (pos == n_tot - 1))
    zz = z + mup_ref[...] * (prev - z) + mun_ref[...] * (nxt - z)
    r, k, v, lo = zz[:, 0:w], zz[:, w:2 * w], zz[:, 2 * w:3 * w], zz[:, 3 * w:]
    seg_j = j_ref[...]
    kk = k * kk_ref[...]
    kk = kk * lax.rsqrt(_segsum(kk * kk, seg_j) + 1e-12)
    w_log = -_softplus(-(_mx(jnp.tanh(lo), w2_ref[...], NN, 3) + w0_ref[...])) - 0.5
    log_decay = -jnp.exp(w_log)
    a = jax.nn.sigmoid(_mx(lo, a2_ref[...], NN, 3) + a0_ref[...])
    ri = lax.broadcasted_iota(jnp.int32, (tm, tm), 0)
    ci = lax.broadcasted_iota(jnp.int32, (tm, tm), 1)
    same = (ri // RW_C) == (ci // RW_C)
    tot_m = jnp.where(same, 1.0, 0.0).astype(BF16)
    bonus = jnp.zeros((tm, w), F32)
    for d in range(2):
        ld = log_decay[:, d * w:(d + 1) * w]
        ad = a[:, d * w:(d + 1) * w]
        kd = k * (1.0 + (ad - 1.0) * ka_ref[...])
        b = kk * ad
        bonus = bonus + r * kd * rk_ref[...]
        tri = jnp.where(same & ((ci <= ri) if d == 0 else (ci >= ri)), 1.0, 0.0).astype(BF16)
        cum = _mx_exact_lhs(tri, ld)
        tot = _mx_exact_lhs(tot_m, ld)
        e_neg = jnp.exp(-cum)
        e_hat = jnp.exp(tot - cum)
        feats = (kk * jnp.exp(cum - ld), b * e_neg, kd * e_neg, r * jnp.exp(cum), kd * e_hat, b * e_hat)
        for q in range(RW_FEATS):
            for h in range(RW_HEADS):
                f_out[0, d, q, h] = feats[q][:, h * RW_N:(h + 1) * RW_N]
        g = jnp.exp(tot)
        gc_out[0, d, 0] = jnp.concatenate([g[c * RW_C:c * RW_C + 1] for c in range(tm // RW_C)], axis=0)
    for h in range(RW_HEADS):
        v_out[0, h] = v[:, h * RW_N:(h + 1) * RW_N]
    bv_out[0] = _segsum(bonus, seg_j) * v


def _rw_features(rw, params, seg_j, n_ctx):
    mu_prev, mu_next, w0, w2, a0, a2, k_k, k_a, r_k = params
    bsz, n, wz = rw.shape
    tm = ROW_TILE
    w = BRANCH_W
    nb8 = n // 8
    lw = 4 * RW_LORA
    w2p = jnp.zeros((lw, 2 * w), F32)
    a2p = jnp.zeros((lw, 2 * w), F32)
    for d in range(2):
        w2p = w2p.at[d * RW_LORA:(d + 1) * RW_LORA, d * w:(d + 1) * w].set(w2[d])
        a2p = a2p.at[(2 + d) * RW_LORA:(3 + d) * RW_LORA, d * w:(d + 1) * w].set(a2[d])
    vec = lambda t: t.reshape(1, -1)
    full = lambda shape: pl.BlockSpec(shape, lambda b, j: (0,) * len(shape))
    gpt = tm // RW_C
    return pl.pallas_call(
        functools.partial(_rw_feat_kernel, tm=tm, n_ctx=n_ctx, n_tot=n),
        grid=(bsz, n // tm),
        in_specs=[pl.BlockSpec((1, tm, wz), lambda b, j: (b, j, 0)),
                  pl.BlockSpec((1, 8, wz), lambda b, j: (b, jnp.maximum(j * (tm // 8) - 1, 0), 0)),
                  pl.BlockSpec((1, 8, wz), lambda b, j: (b, jnp.minimum((j + 1) * (tm // 8), nb8 - 1), 0)),
                  full((1, wz)), full((1, wz)), full((lw, 2 * w)), full((lw, 2 * w)),
                  full((1, 2 * w)), full((1, 2 * w)), full((1, w)), full((1, w)), full((1, w)), full((w, w))],
        out_specs=[pl.BlockSpec((1, 2, RW_FEATS, RW_HEADS, tm, RW_N), lambda b, j: (b, 0, 0, 0, j, 0)),
                   pl.BlockSpec((1, RW_HEADS, tm, RW_N), lambda b, j: (b, 0, j, 0)),
                   pl.BlockSpec((1, 2, 1, gpt, w), lambda b, j: (b, 0, j, 0, 0)),
                   pl.BlockSpec((1, tm, w), lambda b, j: (b, j, 0))],
        out_shape=[jax.ShapeDtypeStruct((bsz, 2, RW_FEATS, RW_HEADS, n, RW_N), F32),
                   jax.ShapeDtypeStruct((bsz, RW_HEADS, n, RW_N), F32),
                   jax.ShapeDtypeStruct((bsz, 2, n // tm, gpt, w), F32),
                   jax.ShapeDtypeStruct((bsz, n, w), F32)],
        compiler_params=_cparams(("parallel", "parallel"), 56),
        name="rwkv_features",
    )(rw, rw, rw, vec(mu_prev), vec(mu_next), w2p, a2p, vec(w0), vec(a0), vec(k_k), vec(k_a), vec(r_k), seg_j)


def _tri_inverse(a_list, ri, ci):
    mm = lambda ps, qs: [_mx(p, q, NN, RW_PASSES_INV) for p, q in zip(ps, qs)]
    eye = jnp.where(ri == ci, 1.0, 0.0)
    x = [jnp.where((ri // 8) == (ci // 8), -a, 0.0) for a in a_list]
    p = [eye + t for t in x]
    x2 = mm(x, x)
    p = [s + t for s, t in zip(p, mm(p, x2))]
    x4 = mm(x2, x2)
    p = [s + t for s, t in zip(p, mm(p, x4))]
    blk = 8
    while blk < RW_C:
        off = ((ri // (2 * blk)) == (ci // (2 * blk))) & ((ri // blk) != (ci // blk))
        e = [jnp.where(off, a, 0.0) for a in a_list]
        p = [s - t for s, t in zip(p, mm(mm(p, e), p))]
        blk *= 2
    return p


def _rw_chunk_kernel(f_ref, v_ref, gc_ref, mn_out, q_out, o_out):
    c = pl.program_id(1)
    gpt = gc_ref.shape[3]
    nk = RW_N
    ri = lax.broadcasted_iota(jnp.int32, (RW_C, RW_C), 0)
    ci = lax.broadcasted_iota(jnp.int32, (RW_C, RW_C), 1)
    eye = ri == ci
    before = ((ci < ri), (ci > ri))
    upto = tuple(m | eye for m in before)
    g_all = [gc_ref[0, d, 0, pl.ds(c % gpt, 1), :] for d in range(2)]
    ent = [(d, h) for d in range(2) for h in range(RW_HEADS)]
    idx = range(len(ent))
    kap, bet, kt, rt, kh, bh = ([f_ref[0, d, q, h] for d, h in ent] for q in range(RW_FEATS))
    v = [v_ref[0, h] for _, h in ent]
    kr = [jnp.concatenate([kap[i], rt[i]], axis=0) for i in idx]
    sb = [_mx(kr[i], bet[i], NT, RW_PASSES_SCORE) for i in idx]
    sk = [_mx(kr[i], kt[i], NT, RW_PASSES_SCORE) for i in idx]
    a_ab = [jnp.where(before[ent[i][0]], sb[i][0:RW_C], 0.0) for i in idx]
    a_qb = [jnp.where(upto[ent[i][0]], sb[i][RW_C:], 0.0) for i in idx]
    a_k = [jnp.concatenate([jnp.where(before[ent[i][0]], sk[i][0:RW_C], 0.0),
                            jnp.where(upto[ent[i][0]], sk[i][RW_C:], 0.0)], axis=0) for i in idx]
    akv = [_mx(a_k[i], v[i], NN, RW_PASSES_MIX) for i in idx]
    vk = [_mx(v[i], kh[i], TN, RW_PASSES_MIX) for i in idx]
    t_inv = _tri_inverse(a_ab, ri, ci)
    wu = [_mx(t_inv[i], jnp.concatenate([kap[i], akv[i][0:RW_C]], axis=1), NN, RW_PASSES_MIX)
          for i in idx]
    wub = [_mx(wu[i], bh[i], TN, RW_PASSES_MIX) for i in idx]
    aq = [_mx(a_qb[i], wu[i], NN, RW_PASSES_MIX) for i in idx]
    for i, (d, h) in enumerate(ent):
        g = g_all[d][:, h * nk:(h + 1) * nk]
        mn_out[0, d, 0, h, 0] = jnp.where(eye, g, 0.0) - wub[i][0:nk]
        mn_out[0, d, 1, h, 0] = vk[i] - wub[i][nk:]
        q_out[0, d, h] = (rt[i] - aq[i][:, 0:nk]).astype(q_out.dtype)
        o_out[0, d, h] = akv[i][RW_C:] - aq[i][:, nk:]


def _rw_chunks(feats, v, gc):
    bsz, _, _, _, n, _ = feats.shape
    nch = n // RW_C
    gpt = gc.shape[3]
    q_dtype = BF16 if RW_PASSES_OUT == 1 else F32
    return pl.pallas_call(
        _rw_chunk_kernel,
        grid=(bsz, nch),
        in_specs=[pl.BlockSpec((1, 2, RW_FEATS, RW_HEADS, RW_C, RW_N), lambda b, c: (b, 0, 0, 0, c, 0)),
                  pl.BlockSpec((1, RW_HEADS, RW_C, RW_N), lambda b, c: (b, 0, c, 0)),
                  pl.BlockSpec((1, 2, 1, gpt, BRANCH_W), lambda b, c: (b, 0, c // gpt, 0, 0))],
        out_specs=[pl.BlockSpec((1, 2, 2, RW_HEADS, 1, RW_N, RW_N), lambda b, c: (b, 0, 0, 0, c, 0, 0)),
                   pl.BlockSpec((1, 2, RW_HEADS, RW_C, RW_N), lambda b, c: (b, 0, 0, c, 0)),
                   pl.BlockSpec((1, 2, RW_HEADS, RW_C, RW_N), lambda b, c: (b, 0, 0, c, 0))],
        out_shape=[jax.ShapeDtypeStruct((bsz, 2, 2, RW_HEADS, nch, RW_N, RW_N), F32),
                   jax.ShapeDtypeStruct((bsz, 2, RW_HEADS, n, RW_N), q_dtype),
                   jax.ShapeDtypeStruct((bsz, 2, RW_HEADS, n, RW_N), F32)],
        compiler_params=_cparams(("parallel", "parallel"), 32),
        name="rwkv_chunks",
    )(feats, v, gc)


def _rw_scan_kernel(mnf_ref, mnb_ref, qf_ref, qb_ref, of_ref, ob_ref, yf_ref, yb_ref, state):
    @pl.when(pl.program_id(1) == 0)
    def _():
        state[...] = jnp.zeros_like(state)

    for d, (mn_ref, q_ref, o_ref, y_ref) in enumerate(((mnf_ref, qf_ref, of_ref, yf_ref),
                                                       (mnb_ref, qb_ref, ob_ref, yb_ref))):
        outs = []
        for h in range(RW_HEADS):
            s0 = state[d, h]
            outs.append(_mx(q_ref[0, 0, h], s0, NT, RW_PASSES_OUT) + o_ref[0, 0, h])
            state[d, h] = _mx(s0, mn_ref[0, 0, 0, h, 0], NN, RW_PASSES_STATE) + mn_ref[0, 0, 1, h, 0]
        y_ref[0] = jnp.concatenate(outs, axis=1)


def _rw_scan(mn, q, o, n_ctx):
    bsz, _, _, _, nch, _, _ = mn.shape
    nch_ctx = n_ctx // RW_C
    n = nch * RW_C

    def rv(s):
        return jnp.where(s < nch_ctx, nch_ctx - 1 - s, nch + nch_ctx - 1 - s)

    mn_spec = lambda d, f: pl.BlockSpec((1, 1, 2, RW_HEADS, 1, RW_N, RW_N), lambda b, s: (b, d, 0, 0, f(s), 0, 0))
    qo_spec = lambda d, f: pl.BlockSpec((1, 1, RW_HEADS, RW_C, RW_N), lambda b, s: (b, d, 0, f(s), 0))
    ident = lambda s: s
    return pl.pallas_call(
        _rw_scan_kernel,
        grid=(bsz, nch),
        in_specs=[mn_spec(0, ident), mn_spec(1, rv), qo_spec(0, ident), qo_spec(1, rv), qo_spec(0, ident),
                  qo_spec(1, rv)],
        out_specs=[pl.BlockSpec((1, RW_C, BRANCH_W), lambda b, s: (b, s, 0)),
                   pl.BlockSpec((1, RW_C, BRANCH_W), lambda b, s: (b, rv(s), 0))],
        out_shape=[jax.ShapeDtypeStruct((bsz, n, BRANCH_W), F32), jax.ShapeDtypeStruct((bsz, n, BRANCH_W), F32)],
        scratch_shapes=[pltpu.VMEM((2, RW_HEADS, RW_N, RW_N), F32)],
        compiler_params=_cparams(("parallel", "arbitrary"), 32),
        name="rwkv_scan",
    )(mn, mn, q, q, o, o)


def _conv3_kernel(x_ref, p_ref, n_ref, w_ref, b_ref, o_ref, *, tm, n_blocks):
    j = pl.program_id(1)
    z = x_ref[0]
    rowi = lax.broadcasted_iota(jnp.int32, (tm, 1), 0)
    prev, nxt = _shift_rows(z, p_ref[0, 7:8, :], n_ref[0, 0:1, :],
                            (rowi == 0) & (j == 0), (rowi == tm - 1) & (j == n_blocks - 1))
    o_ref[0] = prev * w_ref[0:1, :] + z * w_ref[1:2, :] + nxt * w_ref[2:3, :] + b_ref[...]


def _hy_conv3(hy, conv_w, conv_b, seg_start, seg_len):
    bsz, _, w = hy.shape
    tm = ROW_TILE
    off, nb = seg_start // tm, seg_len // tm
    off8, nb8 = seg_start // 8, seg_len // 8
    return pl.pallas_call(
        functools.partial(_conv3_kernel, tm=tm, n_blocks=nb),
        grid=(bsz, nb),
        in_specs=[pl.BlockSpec((1, tm, w), lambda b, j: (b, off + j, 0)),
                  pl.BlockSpec((1, 8, w), lambda b, j: (b, off8 + jnp.maximum(j * (tm // 8) - 1, 0), 0)),
                  pl.BlockSpec((1, 8, w), lambda b, j: (b, off8 + jnp.minimum((j + 1) * (tm // 8), nb8 - 1), 0)),
                  pl.BlockSpec((3, w), lambda b, j: (0, 0)),
                  pl.BlockSpec((1, w), lambda b, j: (0, 0))],
        out_specs=pl.BlockSpec((1, tm, w), lambda b, j: (b, j, 0)),
        out_shape=jax.ShapeDtypeStruct((bsz, seg_len, w), F32),
        compiler_params=_cparams(("parallel", "parallel"), 32),
        name="hyena_conv3",
    )(hy, hy, hy, conv_w, conv_b.reshape(1, w))


def _hy_embedding(n):
    t = jnp.linspace(0.0, 1.0, n, dtype=F32)[:, None]
    ang = 2.0 * math.pi * jnp.arange(n, dtype=F32)[:, None] / n
    bands = jnp.linspace(1e-4, HY_BANDS - 1, HY_BANDS, dtype=F32)[None, :]
    z = jnp.concatenate([t, jnp.cos(bands * ang), -jnp.sin(bands * ang)], axis=-1)
    return jnp.pad(z, ((0, 0), (0, HY_FFN - HY_EMB)))


def _hy_filter_kernel(e_ref, w1_ref, b1_ref, f1_ref, w2_ref, b2_ref, f2_ref, w3_ref, dl_ref, o_ref):
    e = e_ref[...]
    h = jnp.sin(f1_ref[...] * (_dot(e, w1_ref[...], HI) + b1_ref[...]))
    h = jnp.sin(f2_ref[...] * (_dot(h, w2_ref[...], HI) + b2_ref[...]))
    o_ref[...] = _dot(h, w3_ref[...], HI) * jnp.exp(-e[:, 0:1] * dl_ref[...])


def _hy_filters(n, w1, b1, f1, w2, b2, f2, w3):
    tm = min(ROW_TILE, n)
    wf = w3.shape[1]
    emb = _hy_embedding(n)
    w1p = jnp.pad(w1, ((0, HY_FFN - HY_EMB), (0, 0)))
    deltas = jnp.abs(jnp.linspace(math.log(HY_TARGET) / HY_SLOW_PCT, math.log(HY_TARGET) / HY_FAST_PCT,
                                  BRANCH_W, dtype=F32))
    dl = jnp.tile(deltas, wf // BRANCH_W).reshape(1, wf)
    vec = lambda t: t.reshape(1, -1)
    full = lambda shape: pl.BlockSpec(shape, lambda i: (0,) * len(shape))
    return pl.pallas_call(
        _hy_filter_kernel,
        grid=(n // tm,),
        in_specs=[pl.BlockSpec((tm, HY_FFN), lambda i: (i, 0)),
                  full((HY_FFN, HY_FFN)), full((1, HY_FFN)), full((1, HY_FFN)),
                  full((HY_FFN, HY_FFN)), full((1, HY_FFN)), full((1, HY_FFN)),
                  full((HY_FFN, wf)), full((1, wf))],
        out_specs=pl.BlockSpec((tm, wf), lambda i: (i, 0)),
        out_shape=jax.ShapeDtypeStruct((n, wf), F32),
        compiler_params=_cparams(("parallel",), 32),
        name="hyena_filters",
    )(emb, w1p, vec(b1), vec(f1), w2, vec(b2), vec(f2), w3, dl)


def _dft_mats(n):
    k = jnp.arange(n, dtype=jnp.int32)
    ang = (math.pi / n) * ((k[:, None] * k[None, :]) % (2 * n)).astype(F32)
    alt = jnp.where(k % 2 == 0, 1.0, -1.0).astype(F32)
    cos, sin = jnp.cos(ang), jnp.sin(ang)
    fwd = jnp.concatenate([cos, (-sin).at[0].set(alt)], axis=0)
    wk = jnp.where(k == 0, 1.0, 2.0).astype(F32)[None, :]
    inv = jnp.concatenate([wk * cos, (-2.0 * sin).at[:, 0].set(alt)], axis=1) / (2 * n)
    return fwd.astype(BF16), inv.astype(BF16)


def _hy_spec_kernel(fc_ref, fs_ref, h_ref, g_ref, *, tk):
    h = h_ref[...]
    n, w2 = h.shape
    w = w2 // 2
    rowi = lax.broadcasted_iota(jnp.int32, (n, w2), 0)
    coli = lax.broadcasted_iota(jnp.int32, (n, w2), 1)
    h = jnp.where((rowi == 0) & (coli >= w), 0.0, h)
    hi = h.astype(BF16)
    lo = (h - hi.astype(F32)).astype(BF16)
    sr = _dot(fc_ref[...], hi) + _dot(fc_ref[...], lo)
    si = _dot(fs_ref[...], hi) + _dot(fs_ref[...], lo)
    krow = pl.program_id(1) * tk + lax.broadcasted_iota(jnp.int32, (tk, 1), 0)
    g_ref[0, 0] = sr[:, 0:w] + sr[:, w:w2]
    g_ref[0, 1] = si[:, 0:w] + jnp.where(krow == 0, 1.0, -1.0) * si[:, w:w2]


def _hy_spectrum(filt, fwd, n):
    w = BRANCH_W
    tk = min(ROW_TILE, n)
    nk = n // tk
    return pl.pallas_call(
        functools.partial(_hy_spec_kernel, tk=tk),
        grid=(2, nk),
        in_specs=[pl.BlockSpec((tk, n), lambda j, kt: (kt, 0)),
                  pl.BlockSpec((tk, n), lambda j, kt: (nk + kt, 0)),
                  pl.BlockSpec((n, 2 * w), lambda j, kt: (0, j))],
        out_specs=pl.BlockSpec((1, 2, tk, w), lambda j, kt: (j, 0, kt, 0)),
        out_shape=jax.ShapeDtypeStruct((2, 2, n, w), F32),
        compiler_params=_cparams(("parallel", "parallel"), 48),
        name="hyena_filter_spectrum",
    )(fwd, fwd, filt)


def _hy_fwd_kernel(fc_ref, fs_ref, u_ref, g_ref, z_ref):
    u = u_ref[0].astype(BF16)
    ur = _dot(fc_ref[...], u)
    ui = _dot(fs_ref[...], u)
    gr, gi = g_ref[0], g_ref[1]
    tk = ur.shape[0]
    first = (pl.program_id(1) == 0) & (lax.broadcasted_iota(jnp.int32, (tk, 1), 0) == 0)
    z_ref[0, 0] = jnp.where(first, ur * gr, ur * gr - ui * gi).astype(BF16)
    z_ref[0, 1] = jnp.where(first, ui * gi, ur * gi + ui * gr).astype(BF16)


def _hy_inv_kernel(gi_ref, z_ref, u_ref, x_ref, skip_ref, o_ref):
    y = _dot(gi_ref[...], z_ref[0])
    o_ref[0] = x_ref[0] * (y + skip_ref[...] * u_ref[0])


def _hy_long_conv(src, src_col, gate_src, gate_col, spec_j, skip_j, fwd, inv):
    bsz, n, _ = src.shape
    w = BRANCH_W
    tk = min(ROW_TILE, n)
    nk = n // tk
    z = pl.pallas_call(
        _hy_fwd_kernel,
        grid=(bsz, nk),
        in_specs=[pl.BlockSpec((tk, n), lambda b, kt: (kt, 0)),
                  pl.BlockSpec((tk, n), lambda b, kt: (nk + kt, 0)),
                  pl.BlockSpec((1, n, w), lambda b, kt: (b, 0, src_col)),
                  pl.BlockSpec((2, tk, w), lambda b, kt: (0, kt, 0))],
        out_specs=pl.BlockSpec((1, 2, tk, w), lambda b, kt: (b, 0, kt, 0)),
        out_shape=jax.ShapeDtypeStruct((bsz, 2, n, w), BF16),
        compiler_params=_cparams(("parallel", "parallel"), 48),
        name="hyena_dft",
    )(fwd, fwd, src, spec_j)
    z = z.reshape(bsz, 2 * n, w)
    return pl.pallas_call(
        _hy_inv_kernel,
        grid=(bsz, nk),
        in_specs=[pl.BlockSpec((tk, 2 * n), lambda b, t: (t, 0)),
                  pl.BlockSpec((1, 2 * n, w), lambda b, t: (b, 0, 0)),
                  pl.BlockSpec((1, tk, w), lambda b, t: (b, t, src_col)),
                  pl.BlockSpec((1, tk, w), lambda b, t: (b, t, gate_col)),
                  pl.BlockSpec((1, w), lambda b, t: (0, 0))],
        out_specs=pl.BlockSpec((1, tk, w), lambda b, t: (b, t, 0)),
        out_shape=jax.ShapeDtypeStruct((bsz, n, w), F32),
        compiler_params=_cparams(("parallel", "parallel"), 48),
        name="hyena_idft",
    )(inv, z, src, gate_src, skip_j.reshape(1, w))


def _hyena_segment(hy, seg_start, seg_len, conv_w, conv_b, ffn, skip):
    z = _hy_conv3(hy, conv_w, conv_b, seg_start, seg_len)
    filt = _hy_filters(seg_len, *ffn)
    fwd, inv = _dft_mats(seg_len)
    spec = _hy_spectrum(filt, fwd, seg_len)
    y1 = _hy_long_conv(z, 0, z, 1, spec[0], skip[0], fwd, inv)
    return _hy_long_conv(y1, 0, z, 2, spec[1], skip[1], fwd, inv)


def _merge_kernel(x_ref, gx_ref, gc_ref, s5u_ref, s5y_ref, d_ref, gw_ref, gb_ref, att_ref,
                  rwf_ref, rwb_ref, bv_ref, lng_ref, lnb_ref, hyl_ref, hyc_ref, gate_ref, bg_ref, wo_ref, j_ref,
                  o_ref, *, n_ctx_blocks, blk_off):
    w = BRANCH_W
    is_ctx = (pl.program_id(1) + blk_off) < n_ctx_blocks

    def rms(y, g):
        return y * lax.rsqrt(jnp.mean(y * y, axis=-1, keepdims=True) + NORM_EPS) * g

    ys = d_ref[...] * s5u_ref[0] + s5y_ref[0]
    ys = 0.5 * ys * (1.0 + lax.erf(ys * (2.0 ** -0.5)))
    ys = ys * jax.nn.sigmoid(_dot(ys.astype(BF16), gw_ref[...]) + gb_ref[...])
    ys = rms(ys, bg_ref[0:1, :])
    ya = rms(att_ref[0], bg_ref[1:2, :])
    seg_j = j_ref[...]
    yr = rwf_ref[0] + rwb_ref[0]
    mu = _segsum(yr, seg_j) * (1.0 / RW_N)
    yc = yr - mu
    var = _segsum(yc * yc, seg_j) * (1.0 / RW_N)
    yr = yc * lax.rsqrt(var + RW_LN_EPS) * lng_ref[...] + lnb_ref[...] + bv_ref[0]
    yh = rms(jnp.where(is_ctx, hyc_ref[0], hyl_ref[0]), bg_ref[2:3, :])
    gp = gate_ref[0]
    sg = gp * jax.nn.sigmoid(gp)
    acc = _dot((ys * sg[:, 0:w]).astype(BF16), wo_ref[0:w, :])
    acc += _dot((ya * sg[:, w:2 * w]).astype(BF16), wo_ref[w:2 * w, :])
    acc += _dot((yr * sg[:, 2 * w:3 * w]).astype(BF16), wo_ref[2 * w:3 * w, :])
    acc += _dot((yh * sg[:, 3 * w:4 * w]).astype(BF16), wo_ref[3 * w:4 * w, :])
    gate = jnp.where(is_ctx, gc_ref[0, 0], gx_ref[0, 0])
    o_ref[0] = x_ref[0] + gate * acc


def _merge(x_all, mod_l, s5u, s5y, d_skip, glu_w, glu_b, att, rwf, rwb, bv, ln_g, ln_b, hy_lat, hy_ctx,
           gate_pre, branch_g, w_out, seg_j, n_ctx, with_ctx):
    bsz, n, d = x_all.shape
    w = BRANCH_W
    tm = ROW_TILE
    ncb = n_ctx // tm
    off = 0 if with_ctx else ncb
    nb = n // tm - off
    nlb = (n - n_ctx) // tm
    ctx_row = bsz
    row = lambda width: pl.BlockSpec((1, tm, width), lambda b, j: (b, j + off, 0))
    full = lambda shape: pl.BlockSpec(shape, lambda b, j: (0,) * len(shape))
    if hy_ctx is None:
        hy_ctx = hy_lat
        hyc_spec = pl.BlockSpec((1, tm, w), lambda b, j: (b, 0, 0))
    else:
        hyc_spec = pl.BlockSpec((1, tm, w), lambda b, j: (b, jnp.minimum(j + off, ncb - 1), 0))
    vec = lambda t: t.reshape(1, -1)
    return pl.pallas_call(
        functools.partial(_merge_kernel, n_ctx_blocks=ncb, blk_off=off),
        grid=(bsz, nb),
        in_specs=[row(d),
                  pl.BlockSpec((1, 1, 1, d), lambda b, j: (b, 2, 0, 0)),
                  pl.BlockSpec((1, 1, 1, d), lambda b, j: (ctx_row, 2, 0, 0)),
                  row(w), row(w), full((1, w)), full((w, w)), full((1, w)),
                  row(w), row(w), row(w), row(w), full((1, w)), full((1, w)),
                  pl.BlockSpec((1, tm, w), lambda b, j: (b, jnp.clip(j + off - ncb, 0, nlb - 1), 0)),
                  hyc_spec, row(4 * w), full((3, w)), full((4 * w, d)), full((w, w))],
        out_specs=pl.BlockSpec((1, tm, d), lambda b, j: (b, j, 0)),
        out_shape=jax.ShapeDtypeStruct((bsz, nb * tm, d), F32),
        compiler_params=_cparams(("parallel", "parallel"), 56),
        name="merge_out_proj",
    )(x_all, mod_l, mod_l, s5u, s5y, vec(d_skip), glu_w.astype(BF16), vec(glu_b), att,
      rwf, rwb, bv, vec(ln_g), vec(ln_b), hy_lat, hy_ctx, gate_pre, branch_g, w_out, seg_j)


def kernel(x, c, ctx, c_ctx, norm_g, w_ada, b_ada, w_in, w_out, branch_g, s5_lam_re, s5_lam_im, s5_log_step, s5_b_re, s5_b_im, s5_c_re, s5_c_im, s5_d, s5_glu_w, s5_glu_b, att_q_g, att_k_g, att_sink, rw_mu_prev, rw_mu_next, rw_w0, rw_w2, rw_a0, rw_a2, rw_k_k, rw_k_a, rw_r_k, rw_ln_g, rw_ln_b, hy_conv_w, hy_conv_b, hy_w1, hy_b1, hy_f1, hy_w2, hy_b2, hy_f2, hy_w3, hy_skip):
    bsz, n_lat, d = x.shape
    n_ctx = ctx.shape[1]
    depth = w_ada.shape[0]
    n = n_ctx + n_lat
    assert bsz + 1 <= MOD_ROWS and n_ctx % ROW_TILE == 0 and n_lat % ROW_TILE == 0

    c_all = jnp.zeros((MOD_ROWS, d), F32).at[0:bsz].set(c).at[bsz].set(c_ctx)
    mod = _ada_mod(c_all, w_ada, b_ada).reshape(depth, MOD_ROWS, 3, 1, d)
    w_in_b = w_in.astype(BF16)
    w_out_b = w_out.astype(BF16)
    rope_cos, rope_sin = _rope_tables(n_ctx, n_lat)
    lane = jnp.arange(BRANCH_W)
    seg_j = (lane[:, None] // HEAD_DIM == lane[None, :] // HEAD_DIM).astype(BF16)
    n_scan_steps = max(1, math.ceil(math.log2(n // S5_T)))
    slabs = (P_S5, P_QKV, P_RW, P_HY, P_GATE)
    tiles = (P_S5, P_QKV, P_RW, P_HY // 2, P_GATE // 2)
    offs = [sum(slabs[:i]) for i in range(len(slabs))]

    x_all = jnp.concatenate([ctx, x], axis=1)
    for l in range(depth):
        with_ctx = l < depth - 1
        h = _norm_mod(x_all, norm_g[l], mod[l], n_ctx).reshape(bsz * n, d)
        s5u, qkv, rw, hy, gate_pre = (
            _mm(h, w_in_b[l][:, o:o + wd], 512, tn).reshape(bsz, n, wd) for o, wd, tn in zip(offs, slabs, tiles))

        s5_w = _s5_weights(s5_lam_re[l], s5_lam_im[l], s5_log_step[l], s5_b_re[l], s5_b_im[l],
                           s5_c_re[l], s5_c_im[l], n_scan_steps, n_lat // S5_T, bsz)
        s5y = _s5_mix(s5u, s5_w, n_ctx)

        qh, kh, vh = _qk_prep(qkv, rope_cos, rope_sin, att_q_g[l], att_k_g[l], seg_j)
        att = _attn(qh, kh, vh, att_sink[l], n_ctx)

        feats, rv, gc, bv = _rw_features(
            rw, (rw_mu_prev[l], rw_mu_next[l], rw_w0[l], rw_w2[l], rw_a0[l], rw_a2[l], rw_k_k[l], rw_k_a[l],
                 rw_r_k[l].reshape(-1)), seg_j, n_ctx)
        mn, rq, ro = _rw_chunks(feats, rv, gc)
        rwf, rwb = _rw_scan(mn, rq, ro, n_ctx)

        ffn = (hy_w1[l], hy_b1[l], hy_f1[l], hy_w2[l], hy_b2[l], hy_f2[l], hy_w3[l])
        hy_lat = _hyena_segment(hy, n_ctx, n_lat, hy_conv_w[l], hy_conv_b[l], ffn, hy_skip[l])
        hy_ctx = _hyena_segment(hy, 0, n_ctx, hy_conv_w[l], hy_conv_b[l], ffn, hy_skip[l]) if with_ctx else None

        x_all = _merge(x_all, mod[l], s5u, s5y, s5_d[l].reshape(-1), s5_glu_w[l], s5_glu_b[l], att,
                       rwf, rwb, bv, rw_ln_g[l], rw_ln_b[l], hy_lat, hy_ctx, gate_pre, branch_g[l], w_out_b[l],
                       seg_j, n_ctx, with_ctx)
    return x_all
```

```python
import functools
import math

import jax
import jax.numpy as jnp
from jax import lax
from jax.experimental import pallas as pl
from jax.experimental.pallas import tpu as pltpu

F32 = jnp.float32
BF16 = jnp.bfloat16
HI = lax.Precision.HIGHEST

D_MODEL = 2048
GRID_W = 64
BRANCH_W = 512
NORM_EPS = 1e-6
S5_H = 16
S5_G = BRANCH_W // S5_H
S5_P = 64
HEAD_DIM = 64
ATT_HEADS = 8
ATT_KV_HEADS = 2
ATT_REP = ATT_HEADS // ATT_KV_HEADS
WINDOW = 128
BLOCK = 128
ROPE_BASE = 10000.0
NEG_INF = -1e30
RW_N = 64
RW_HEADS = 8
RW_LORA = 32
RW_LN_EPS = 64e-5
HY_EMB = 33
HY_BANDS = 16
HY_FFN = 64
HY_TARGET = 1e-2
HY_FAST_PCT = 0.3
HY_SLOW_PCT = 1.5
P_S5 = BRANCH_W
P_QKV = ATT_HEADS * HEAD_DIM + 2 * ATT_KV_HEADS * HEAD_DIM
P_RW = 3 * BRANCH_W + 4 * RW_LORA
P_HY = 3 * BRANCH_W
P_GATE = 4 * BRANCH_W

V7X_VMEM_BYTES = 64 * 1024 * 1024
S5_T = 16
RW_C = 64
ROW_TILE = 256
MOD_ROWS = 24


def _cparams(sem, vmem_mb):
    assert vmem_mb * 1024 * 1024 <= V7X_VMEM_BYTES
    return pltpu.CompilerParams(dimension_semantics=sem, vmem_limit_bytes=vmem_mb * 1024 * 1024)


def _dot(a, b, precision=None):
    return jnp.dot(a, b, precision=precision, preferred_element_type=F32)


def _dot_nt(a, b, precision=None):
    return lax.dot_general(a, b, (((1,), (1,)), ((), ())), precision=precision, preferred_element_type=F32)


def _dot_tn(a, b, precision=None):
    return lax.dot_general(a, b, (((0,), (0,)), ((), ())), precision=precision, preferred_element_type=F32)


NN = ((1,), (0,))
NT = ((1,), (1,))
TN = ((0,), (0,))


def _split(x):
    hi = x.astype(BF16)
    return hi, (x - hi.astype(F32)).astype(BF16)


def _mx(a, b, dims, passes):
    dg = lambda p, q: lax.dot_general(p, q, (dims, ((), ())), preferred_element_type=F32)
    if passes == 1:
        return dg(a.astype(BF16), b.astype(BF16))
    ah, al = _split(a)
    bh, bl = _split(b)
    return dg(ah, bh) + dg(ah, bl) + dg(al, bh)


def _mx_exact_lhs(a_bf16, b):
    bh, bl = _split(b)
    return _dot(a_bf16, bh) + _dot(a_bf16, bl)


def _segsum(x, j):
    hi, lo = _split(x)
    return _dot(hi, j) + _dot(lo, j)


def _shift_rows(z, prev_row, next_row, zero_prev, zero_next):
    tm = z.shape[0]
    rowi = lax.broadcasted_iota(jnp.int32, (tm, 1), 0)
    prev = jnp.where(rowi == 0, prev_row, pltpu.roll(z, 1, 0))
    nxt = jnp.where(rowi == tm - 1, next_row, pltpu.roll(z, tm - 1, 0))
    prev = jnp.where(zero_prev, 0.0, prev)
    nxt = jnp.where(zero_next, 0.0, nxt)
    return prev, nxt


def _mod_kernel(c_ref, w_ref, b_ref, o_ref):
    c = c_ref[...]
    s = c * jax.nn.sigmoid(c)
    o_ref[0] = _dot(s, w_ref[0], HI) + b_ref[0]


def _ada_mod(c_all, w_ada, b_ada):
    depth, d, d3 = w_ada.shape
    tn = 512
    return pl.pallas_call(
        _mod_kernel,
        grid=(depth, d3 // tn),
        in_specs=[pl.BlockSpec((MOD_ROWS, d), lambda l, j: (0, 0)),
                  pl.BlockSpec((1, d, tn), lambda l, j: (l, 0, j)),
                  pl.BlockSpec((1, 1, tn), lambda l, j: (l, 0, j))],
        out_specs=pl.BlockSpec((1, MOD_ROWS, tn), lambda l, j: (l, 0, j)),
        out_shape=jax.ShapeDtypeStruct((depth, MOD_ROWS, d3), F32),
        compiler_params=_cparams(("parallel", "parallel"), 32),
        name="ada_mod",
    )(c_all, w_ada, b_ada.reshape(depth, 1, d3))


def _norm_mod_kernel(x_ref, g_ref, shx_ref, scx_ref, shc_ref, scc_ref, o_ref, *, n_ctx_blocks):
    x = x_ref[0]
    ms = jnp.mean(x * x, axis=-1, keepdims=True)
    y = x * lax.rsqrt(ms + NORM_EPS) * g_ref[...]
    is_ctx = pl.program_id(1) < n_ctx_blocks
    sc = jnp.where(is_ctx, scc_ref[0, 0], scx_ref[0, 0])
    sh = jnp.where(is_ctx, shc_ref[0, 0], shx_ref[0, 0])
    o_ref[0] = (y * (1.0 + sc) + sh).astype(BF16)


def _norm_mod(x_all, g, mod_l, n_ctx):
    bsz, n, d = x_all.shape
    tm = ROW_TILE
    ctx_row = bsz
    return pl.pallas_call(
        functools.partial(_norm_mod_kernel, n_ctx_blocks=n_ctx // tm),
        grid=(bsz, n // tm),
        in_specs=[pl.BlockSpec((1, tm, d), lambda b, j: (b, j, 0)),
                  pl.BlockSpec((1, d), lambda b, j: (0, 0)),
                  pl.BlockSpec((1, 1, 1, d), lambda b, j: (b, 0, 0, 0)),
                  pl.BlockSpec((1, 1, 1, d), lambda b, j: (b, 1, 0, 0)),
                  pl.BlockSpec((1, 1, 1, d), lambda b, j: (ctx_row, 0, 0, 0)),
                  pl.BlockSpec((1, 1, 1, d), lambda b, j: (ctx_row, 1, 0, 0))],
        out_specs=pl.BlockSpec((1, tm, d), lambda b, j: (b, j, 0)),
        out_shape=jax.ShapeDtypeStruct((bsz, n, d), BF16),
        compiler_params=_cparams(("parallel", "parallel"), 32),
        name="norm_mod",
    )(x_all, g.reshape(1, d), mod_l, mod_l, mod_l, mod_l)


def _mm_kernel(a_ref, b_ref, o_ref):
    o_ref[...] = _dot(a_ref[...], b_ref[...]).astype(o_ref.dtype)


def _mm(a, b, tm, tn):
    m, k = a.shape
    n = b.shape[1]
    assert m % tm == 0 and n % tn == 0
    return pl.pallas_call(
        _mm_kernel,
        grid=(n // tn, m // tm),
        in_specs=[pl.BlockSpec((tm, k), lambda j, i: (i, 0)),
                  pl.BlockSpec((k, tn), lambda j, i: (0, j))],
        out_specs=pl.BlockSpec((tm, tn), lambda j, i: (i, j)),
        out_shape=jax.ShapeDtypeStruct((m, n), F32),
        compiler_params=_cparams(("parallel", "parallel"), 48),
        name="in_proj",
    )(a, b)


def _s5_weights(lam_re, lam_im, log_step, b_re, b_im, c_re, c_im, n_scan_steps, nc_lat, nc_ctx):
    t_len = S5_T
    step = jnp.exp(log_step)[..., None]
    th_re, th_im = lam_re * step, lam_im * step

    def cpow(k):
        mag = jnp.exp(th_re[..., None] * k)
        ang = th_im[..., None] * k
        return mag * jnp.cos(ang), mag * jnp.sin(ang)

    lb_re, lb_im = (t[..., 0] for t in cpow(jnp.ones((1,), F32)))
    den = lam_re * lam_re + lam_im * lam_im
    nr = lb_re - 1.0
    co_re = (nr * lam_re + lb_im * lam_im) / den
    co_im = (lb_im * lam_re - nr * lam_im) / den
    bb_re = co_re[..., None] * b_re - co_im[..., None] * b_im
    bb_im = co_re[..., None] * b_im + co_im[..., None] * b_re

    lags = jnp.arange(t_len + 1, dtype=F32)
    pw_re, pw_im = cpow(lags)
    x_re = c_re[..., None] * pw_re[:, :, None] - c_im[..., None] * pw_im[:, :, None]
    x_im = c_re[..., None] * pw_im[:, :, None] + c_im[..., None] * pw_re[:, :, None]
    m_k = (jnp.einsum('dgopk,dgpi->dgkio', x_re, bb_re, precision=HI)
           - jnp.einsum('dgopk,dgpi->dgkio', x_im, bb_im, precision=HI))
    s_idx = jnp.arange(t_len)[:, None]
    t_idx = jnp.arange(t_len)[None, :]
    lag_f = t_idx - s_idx
    kin = []
    for d, lag in enumerate((lag_f, -lag_f)):
        blk = m_k[d][:, jnp.clip(lag, 0, t_len)]
        blk = jnp.where((lag >= 0)[None, :, :, None, None], blk, 0.0)
        kin.append(blk.transpose(0, 1, 3, 2, 4).reshape(S5_G, t_len * S5_H, t_len * S5_H))
    kin = jnp.stack(kin)

    tt = jnp.arange(t_len)
    win, wout = [], []
    for d in range(2):
        e_in = (t_len - 1 - tt) if d == 0 else tt
        e_out = (tt + 1) if d == 0 else (t_len - tt)
        pr, pi = pw_re[d][..., e_in], pw_im[d][..., e_in]
        wr = pr[..., None] * bb_re[d][:, :, None] - pi[..., None] * bb_im[d][:, :, None]
        wi = pr[..., None] * bb_im[d][:, :, None] + pi[..., None] * bb_re[d][:, :, None]
        w = jnp.concatenate([wr, wi], axis=1)
        win.append(w.transpose(0, 2, 3, 1).reshape(S5_G, t_len * S5_H, 2 * S5_P))
        qr, qi = pw_re[d][..., e_out], pw_im[d][..., e_out]
        orr = c_re[d].transpose(0, 2, 1)[:, :, None] * qr[..., None] - c_im[d].transpose(0, 2, 1)[:, :, None] * qi[..., None]
        oii = -(c_re[d].transpose(0, 2, 1)[:, :, None] * qi[..., None] + c_im[d].transpose(0, 2, 1)[:, :, None] * qr[..., None])
        o = jnp.concatenate([orr, oii], axis=1)
        wout.append(o.reshape(S5_G, 2 * S5_P, t_len * S5_H))
    win, wout = jnp.stack(win), jnp.stack(wout)

    sc_re, sc_im = cpow(t_len * (2.0 ** jnp.arange(n_scan_steps, dtype=F32)))
    a1 = jnp.concatenate([sc_re, sc_re], axis=2)
    a2 = jnp.concatenate([-sc_im, sc_im], axis=2)
    pw = jnp.stack([a1, a2], axis=-1).transpose(0, 1, 3, 4, 2).reshape(2, S5_G, 2 * n_scan_steps, 2 * S5_P)
    cr, ci_ = (t[1] for t in cpow(t_len * (nc_lat - 1 - jnp.arange(nc_lat, dtype=F32))))
    tab = jnp.stack([jnp.concatenate([cr, cr], axis=1), jnp.concatenate([-ci_, ci_], axis=1)], axis=1)
    tab = jnp.pad(tab.transpose(0, 1, 3, 2), ((0, 0), (0, 0), (nc_ctx, 0), (0, 0)))
    return kin.astype(BF16), win.astype(BF16), wout.astype(BF16), pw, tab


S5_CB = 16
LANES = 128
S5_GPV = LANES // S5_H
S5_NV = BRANCH_W // LANES


def _lane_block_masks():
    blk = lax.broadcasted_iota(jnp.int32, (1, LANES), 1) // S5_H
    return [blk == m for m in range(S5_GPV)]


def _roll_lanes(x, shift):
    shift %= LANES
    return pltpu.roll(x, shift, 1) if shift else x


def _s5_pack_kernel(u_ref, x_ref):
    masks = _lane_block_masks()
    step = S5_T * S5_NV
    q = [[_roll_lanes(u_ref[0, pl.ds(t * S5_NV + j, S5_CB, stride=step), :], S5_H * t) for j in range(S5_NV)]
         for t in range(S5_T)]
    for j in range(S5_NV):
        for r in range(S5_GPV):
            for half in range(S5_T // S5_GPV):
                z = jnp.zeros((S5_CB, LANES), F32)
                for tp in range(S5_GPV):
                    z = jnp.where(masks[(r + tp) % S5_GPV], q[half * S5_GPV + tp][j], z)
                x_ref[j * S5_GPV + r, 0, :, half * LANES:(half + 1) * LANES] = _roll_lanes(z, -S5_H * r).astype(BF16)


def _s5_unpack_kernel(y_ref, o_ref):
    masks = _lane_block_masks()
    step = S5_T * S5_NV
    for j in range(S5_NV):
        for half in range(S5_T // S5_GPV):
            rl = [_roll_lanes(y_ref[j * S5_GPV + r, 0, :, half * LANES:(half + 1) * LANES], S5_H * r)
                  for r in range(S5_GPV)]
            for tp in range(S5_GPV):
                z = jnp.zeros((S5_CB, LANES), F32)
                for r in range(S5_GPV):
                    z = jnp.where(masks[(r + tp) % S5_GPV], rl[r], z)
                t = half * S5_GPV + tp
                o_ref[0, pl.ds(t * S5_NV + j, S5_CB, stride=step), :] = _roll_lanes(z, -S5_H * tp)


def _s5_kernel(x_ref, kin_ref, win_ref, wout_ref, pw_ref, tab_ref, y_ref, *, bsz, nc, nc_ctx, n_steps):
    rows = bsz * nc
    th = S5_T * S5_H
    sw = 2 * S5_P
    x = x_ref[0].reshape(rows, th)
    cidx = lax.broadcasted_iota(jnp.int32, (bsz, nc, sw), 1).reshape(rows, sw)
    seg_hi = jnp.where(cidx < nc_ctx, nc_ctx, nc)

    def cmul(a1, a2, h):
        return a1 * h + a2 * pltpu.roll(h, S5_P, 1)

    h = _dot(x, win_ref[0, 0])
    for i in range(n_steps):
        d = 1 << i
        a1 = pw_ref[0, 0, 2 * i:2 * i + 1, :]
        a2 = pw_ref[0, 0, 2 * i + 1:2 * i + 2, :]
        h = h + jnp.where(cidx >= d, cmul(a1, a2, pltpu.roll(h, d, 0)), 0.0)
    h_start = jnp.where(cidx >= 1, pltpu.roll(h, 1, 0), 0.0)
    y = _dot(x, kin_ref[0, 0]) + _dot(x, kin_ref[1, 0]) + _dot(h_start.astype(BF16), wout_ref[0, 0])

    h = _dot(x, win_ref[1, 0])
    for i in range(n_steps):
        d = 1 << i
        a1 = pw_ref[1, 0, 2 * i:2 * i + 1, :]
        a2 = pw_ref[1, 0, 2 * i + 1:2 * i + 2, :]
        h = h + jnp.where(cidx + d < seg_hi, cmul(a1, a2, pltpu.roll(h, rows - d, 0)), 0.0)
    h_ctx = jnp.broadcast_to(h.reshape(bsz, nc, sw)[:, 0:1, :], (bsz, nc, sw)).reshape(rows, sw)
    tab1 = jnp.broadcast_to(tab_ref[0, 0][None], (bsz, nc, sw)).reshape(rows, sw)
    tab2 = jnp.broadcast_to(tab_ref[0, 1][None], (bsz, nc, sw)).reshape(rows, sw)
    h_start = jnp.where(cidx + 1 < seg_hi, pltpu.roll(h, rows - 1, 0), 0.0) + cmul(tab1, tab2, h_ctx)
    y = y + _dot(h_start.astype(BF16), wout_ref[1, 0])
    y_ref[0] = y.reshape(bsz, nc, th)


def _s5_mix(u, weights, n_ctx):
    kin, win, wout, pw, tab = weights
    bsz, n, w = u.shape
    nc, nc_ctx = n // S5_T, n_ctx // S5_T
    th = S5_T * S5_H
    n_steps = pw.shape[2] // 2
    assert (1 << n_steps) >= nc and nc % S5_CB == 0 and nc_ctx >= 1
    tok_rows = S5_CB * S5_T * S5_NV
    x = pl.pallas_call(
        _s5_pack_kernel,
        grid=(bsz, nc // S5_CB),
        in_specs=[pl.BlockSpec((1, tok_rows, LANES), lambda b, c: (b, c, 0))],
        out_specs=pl.BlockSpec((S5_G, 1, S5_CB, th), lambda b, c: (0, b, c, 0)),
        out_shape=jax.ShapeDtypeStruct((S5_G, bsz, nc, th), BF16),
        compiler_params=_cparams(("parallel", "parallel"), 32),
        name="s5_pack",
    )(u.reshape(bsz, n * S5_NV, LANES))
    y = pl.pallas_call(
        functools.partial(_s5_kernel, bsz=bsz, nc=nc, nc_ctx=nc_ctx, n_steps=n_steps),
        grid=(S5_G,),
        in_specs=[pl.BlockSpec((1, bsz, nc, th), lambda g: (g, 0, 0, 0)),
                  pl.BlockSpec((2, 1, th, th), lambda g: (0, g, 0, 0)),
                  pl.BlockSpec((2, 1, th, 2 * S5_P), lambda g: (0, g, 0, 0)),
                  pl.BlockSpec((2, 1, 2 * S5_P, th), lambda g: (0, g, 0, 0)),
                  pl.BlockSpec((2, 1, 2 * n_steps, 2 * S5_P), lambda g: (0, g, 0, 0)),
                  pl.BlockSpec((1, 2, nc, 2 * S5_P), lambda g: (g, 0, 0, 0))],
        out_specs=pl.BlockSpec((1, bsz, nc, th), lambda g: (g, 0, 0, 0)),
        out_shape=jax.ShapeDtypeStruct((S5_G, bsz, nc, th), F32),
        compiler_params=_cparams(("parallel",), 40),
        name="s5_chunks",
    )(x, kin, win, wout, pw, tab)
    out = pl.pallas_call(
        _s5_unpack_kernel,
        grid=(bsz, nc // S5_CB),
        in_specs=[pl.BlockSpec((S5_G, 1, S5_CB, th), lambda b, c: (0, b, c, 0))],
        out_specs=pl.BlockSpec((1, tok_rows, LANES), lambda b, c: (b, c, 0)),
        out_shape=jax.ShapeDtypeStruct((bsz, n * S5_NV, LANES), F32),
        compiler_params=_cparams(("parallel", "parallel"), 32),
        name="s5_unpack",
    )(y)
    return out.reshape(bsz, n, w)


def _rope_tables(n_ctx, n_lat):
    quarter = HEAD_DIM // 4
    inv = 1.0 / (ROPE_BASE ** (jnp.arange(quarter, dtype=F32) / quarter))
    t = jnp.arange(n_lat, dtype=jnp.int32)
    ar = (t // GRID_W).astype(F32)[:, None] * inv[None, :]
    ac = (t % GRID_W).astype(F32)[:, None] * inv[None, :]
    cos = jnp.concatenate([jnp.cos(ar), jnp.cos(ar), jnp.cos(ac), jnp.cos(ac)], axis=1)
    sin = jnp.concatenate([-jnp.sin(ar), jnp.sin(ar), -jnp.sin(ac), jnp.sin(ac)], axis=1)
    cos = jnp.concatenate([jnp.ones((n_ctx, HEAD_DIM), F32), cos], axis=0)
    sin = jnp.concatenate([jnp.zeros((n_ctx, HEAD_DIM), F32), sin], axis=0)
    return jnp.concatenate([cos, cos], axis=1), jnp.concatenate([sin, sin], axis=1)


def _qk_prep_kernel(x_ref, cos_ref, sin_ref, qg_ref, kg_ref, j_ref, q_out, k_out, v_out):
    x = x_ref[0]
    wq, wk = ATT_HEADS * HEAD_DIM, ATT_KV_HEADS * HEAD_DIM
    cos, sin = cos_ref[...], sin_ref[...]

    def prep(t, g, w, scale):
        ss = _segsum(t * t, j_ref[0:w, 0:w])
        tn = t * lax.rsqrt(ss * (1.0 / HEAD_DIM) + NORM_EPS) * g
        reps = w // 128
        c = jnp.concatenate([cos] * reps, axis=1) if reps > 1 else cos
        s = jnp.concatenate([sin] * reps, axis=1) if reps > 1 else sin
        lane = lax.broadcasted_iota(jnp.int32, tn.shape, 1)
        first = (lane % 32) < 16
        swapped = jnp.where(first, pltpu.roll(tn, w - 16, 1), pltpu.roll(tn, 16, 1))
        return ((tn * c + swapped * s) * scale).astype(BF16)

    q = prep(x[:, 0:wq], qg_ref[...], wq, HEAD_DIM ** -0.5)
    k = prep(x[:, wq:wq + wk], kg_ref[...], wk, 1.0)
    v = x[:, wq + wk:wq + 2 * wk].astype(BF16)
    for h in range(ATT_HEADS):
        q_out[0, h] = q[:, h * HEAD_DIM:(h + 1) * HEAD_DIM]
    for h in range(ATT_KV_HEADS):
        k_out[0, h] = k[:, h * HEAD_DIM:(h + 1) * HEAD_DIM]
        v_out[0, h] = v[:, h * HEAD_DIM:(h + 1) * HEAD_DIM]


def _qk_prep(qkv, cos, sin, q_g, k_g, seg_j):
    bsz, n, w = qkv.shape
    tm = ROW_TILE
    wq, wk = ATT_HEADS * HEAD_DIM, ATT_KV_HEADS * HEAD_DIM
    return pl.pallas_call(
        _qk_prep_kernel,
        grid=(bsz, n // tm),
        in_specs=[pl.BlockSpec((1, tm, w), lambda b, j: (b, j, 0)),
                  pl.BlockSpec((tm, 128), lambda b, j: (j, 0)),
                  pl.BlockSpec((tm, 128), lambda b, j: (j, 0)),
                  pl.BlockSpec((1, wq), lambda b, j: (0, 0)),
                  pl.BlockSpec((1, wk), lambda b, j: (0, 0)),
                  pl.BlockSpec((BRANCH_W, BRANCH_W), lambda b, j: (0, 0))],
        out_specs=[pl.BlockSpec((1, ATT_HEADS, tm, HEAD_DIM), lambda b, j: (b, 0, j, 0)),
                   pl.BlockSpec((1, ATT_KV_HEADS, tm, HEAD_DIM), lambda b, j: (b, 0, j, 0)),
                   pl.BlockSpec((1, ATT_KV_HEADS, tm, HEAD_DIM), lambda b, j: (b, 0, j, 0))],
        out_shape=[jax.ShapeDtypeStruct((bsz, ATT_HEADS, n, HEAD_DIM), BF16),
                   jax.ShapeDtypeStruct((bsz, ATT_KV_HEADS, n, HEAD_DIM), BF16),
                   jax.ShapeDtypeStruct((bsz, ATT_KV_HEADS, n, HEAD_DIM), BF16)],
        compiler_params=_cparams(("parallel", "parallel"), 32),
        name="qk_prep",
    )(qkv, cos, sin, jnp.tile(q_g, ATT_HEADS).reshape(1, wq), jnp.tile(k_g, ATT_KV_HEADS).reshape(1, wk), seg_j)


def _attn_kernel(sink_ref, q_ref, k_ref, v_ref, o_ref, *, n_ctx, n_lat):
    i = pl.program_id(1)
    n_ctx_blocks = n_ctx // BLOCK
    rows = ATT_REP * BLOCK
    span = 3 * BLOCK
    kv_heads = range(ATT_KV_HEADS)
    rowi = lax.broadcasted_iota(jnp.int32, (rows, 1), 0)
    rep = rowi // BLOCK
    q = [q_ref[0, h * ATT_REP:(h + 1) * ATT_REP].reshape(rows, HEAD_DIM) for h in kv_heads]
    kc = [k_ref[0, h, 0:n_ctx, :] for h in kv_heads]
    vc = [v_ref[0, h, 0:n_ctx, :] for h in kv_heads]
    s_ctx = [_dot_nt(q[h], kc[h]) for h in kv_heads]
    sink = []
    for h in kv_heads:
        s = jnp.zeros((rows, 1), F32)
        for r in range(ATT_REP):
            s = jnp.where(rep == r, sink_ref[h * ATT_REP + r], s)
        sink.append(s)

    def store(o):
        o_ref[0] = jnp.concatenate([o[h][r * BLOCK:(r + 1) * BLOCK] for h in kv_heads for r in range(ATT_REP)], axis=1)

    @pl.when(i < n_ctx_blocks)
    def _():
        m = [jnp.maximum(jnp.max(s_ctx[h], axis=1, keepdims=True), sink[h]) for h in kv_heads]
        p = [jnp.exp(s_ctx[h] - m[h]) for h in kv_heads]
        den = [jnp.sum(p[h], axis=1, keepdims=True) + jnp.exp(sink[h] - m[h]) for h in kv_heads]
        store([_dot(p[h].astype(BF16), vc[h]) / den[h] for h in kv_heads])

    @pl.when(i >= n_ctx_blocks)
    def _():
        il = i - n_ctx_blocks
        start = pl.multiple_of(jnp.clip((il - 1) * BLOCK, 0, n_lat - span), BLOCK)
        qpos = il * BLOCK + rowi % BLOCK
        kpos = start + lax.broadcasted_iota(jnp.int32, (rows, span), 1)
        band = jnp.abs(qpos - kpos) <= WINDOW
        kw = [k_ref[0, h, pl.ds(n_ctx + start, span), :] for h in kv_heads]
        vw = [v_ref[0, h, pl.ds(n_ctx + start, span), :] for h in kv_heads]
        s_loc = [jnp.where(band, _dot_nt(q[h], kw[h]), NEG_INF) for h in kv_heads]
        m = [jnp.maximum(jnp.maximum(jnp.max(s_loc[h], axis=1, keepdims=True), jnp.max(s_ctx[h], axis=1, keepdims=True)),
                         sink[h]) for h in kv_heads]
        p_loc = [jnp.exp(s_loc[h] - m[h]) for h in kv_heads]
        p_ctx = [jnp.exp(s_ctx[h] - m[h]) for h in kv_heads]
        den = [jnp.sum(p_loc[h], axis=1, keepdims=True) + jnp.sum(p_ctx[h], axis=1, keepdims=True)
               + jnp.exp(sink[h] - m[h]) for h in kv_heads]
        store([(_dot(p_loc[h].astype(BF16), vw[h]) + _dot(p_ctx[h].astype(BF16), vc[h])) / den[h] for h in kv_heads])


def _attn(q, k, v, sink, n_ctx):
    bsz, _, n, _ = q.shape
    n_lat = n - n_ctx
    assert n_lat >= 3 * BLOCK and n_ctx % BLOCK == 0
    return pl.pallas_call(
        functools.partial(_attn_kernel, n_ctx=n_ctx, n_lat=n_lat),
        grid=(bsz, n // BLOCK),
        in_specs=[pl.BlockSpec(memory_space=pltpu.SMEM),
                  pl.BlockSpec((1, ATT_HEADS, BLOCK, HEAD_DIM), lambda b, i: (b, 0, i, 0)),
                  pl.BlockSpec((1, ATT_KV_HEADS, n, HEAD_DIM), lambda b, i: (b, 0, 0, 0)),
                  pl.BlockSpec((1, ATT_KV_HEADS, n, HEAD_DIM), lambda b, i: (b, 0, 0, 0))],
        out_specs=pl.BlockSpec((1, BLOCK, ATT_HEADS * HEAD_DIM), lambda b, i: (b, i, 0)),
        out_shape=jax.ShapeDtypeStruct((bsz, n, ATT_HEADS * HEAD_DIM), F32),
        compiler_params=_cparams(("parallel", "parallel"), 32),
        name="window_attn",
    )(sink, q, k, v)


RW_FEATS = 6
RW_PASSES_SCORE = 1
RW_PASSES_INV = 1
RW_PASSES_MIX = 1
RW_PASSES_STATE = 3
RW_PASSES_OUT = 1


def _softplus(y):
    return jnp.maximum(y, 0.0) + jnp.log(1.0 + jnp.exp(-jnp.abs(y)))


def _rw_feat_kernel(x_ref, p_ref, n_ref, mup_ref, mun_ref, w2_ref, a2_ref, w0_ref, a0_ref, kk_ref, ka_ref, rk_ref,
                    j_ref, f_out, v_out, gc_out, bv_out, *, tm, n_ctx, n_tot):
    w = BRANCH_W
    z = x_ref[0]
    rowi = lax.broadcasted_iota(jnp.int32, (tm, 1), 0)
    pos = pl.program_id(1) * tm + rowi
    prev, nxt = _shift_rows(z, p_ref[0, 7:8, :], n_ref[0, 0:1, :],
                            (pos == 0) | (pos == n_ctx), (pos == n_ctx - 1) | (pos == n_tot - 1))
    zz = z + mup_ref[...] * (prev - z) + mun_ref[...] * (nxt - z)
    r, k, v, lo = zz[:, 0:w], zz[:, w:2 * w], zz[:, 2 * w:3 * w], zz[:, 3 * w:]
    seg_j = j_ref[...]
    kk = k * kk_ref[...]
    kk = kk * lax.rsqrt(_segsum(kk * kk, seg_j) + 1e-12)
    w_log = -_softplus(-(_mx(jnp.tanh(lo), w2_ref[...], NN, 3) + w0_ref[...])) - 0.5
    log_decay = -jnp.exp(w_log)
    a = jax.nn.sigmoid(_mx(lo, a2_ref[...], NN, 3) + a0_ref[...])
    ri = lax.broadcasted_iota(jnp.int32, (tm, tm), 0)
    ci = lax.broadcasted_iota(jnp.int32, (tm, tm), 1)
    same = (ri // RW_C) == (ci // RW_C)
    tot_m = jnp.where(same, 1.0, 0.0).astype(BF16)
    bonus = jnp.zeros((tm, w), F32)
    for d in range(2):
        ld = log_decay[:, d * w:(d + 1) * w]
        ad = a[:, d * w:(d + 1) * w]
        kd = k * (1.0 + (ad - 1.0) * ka_ref[...])
        b = kk * ad
        bonus = bonus + r * kd * rk_ref[...]
        tri = jnp.where(same & ((ci <= ri) if d == 0 else (ci >= ri)), 1.0, 0.0).astype(BF16)
        cum = _mx_exact_lhs(tri, ld)
        tot = _mx_exact_lhs(tot_m, ld)
        e_neg = jnp.exp(-cum)
        e_hat = jnp.exp(tot - cum)
        feats = (kk * jnp.exp(cum - ld), b * e_neg, kd * e_neg, r * jnp.exp(cum), kd * e_hat, b * e_hat)
        for q in range(RW_FEATS):
            for h in range(RW_HEADS):
                f_out[0, d, q, h] = feats[q][:, h * RW_N:(h + 1) * RW_N]
        g = jnp.exp(tot)
        gc_out[0, d, 0] = jnp.concatenate([g[c * RW_C:c * RW_C + 1] for c in range(tm // RW_C)], axis=0)
    for h in range(RW_HEADS):
        v_out[0, h] = v[:, h * RW_N:(h + 1) * RW_N]
    bv_out[0] = _segsum(bonus, seg_j) * v


def _rw_features(rw, params, seg_j, n_ctx):
    mu_prev, mu_next, w0, w2, a0, a2, k_k, k_a, r_k = params
    bsz, n, wz = rw.shape
    tm = ROW_TILE
    w = BRANCH_W
    nb8 = n // 8
    lw = 4 * RW_LORA
    w2p = jnp.zeros((lw, 2 * w), F32)
    a2p = jnp.zeros((lw, 2 * w), F32)
    for d in range(2):
        w2p = w2p.at[d * RW_LORA:(d + 1) * RW_LORA, d * w:(d + 1) * w].set(w2[d])
        a2p = a2p.at[(2 + d) * RW_LORA:(3 + d) * RW_LORA, d * w:(d + 1) * w].set(a2[d])
    vec = lambda t: t.reshape(1, -1)
    full = lambda shape: pl.BlockSpec(shape, lambda b, j: (0,) * len(shape))
    gpt = tm // RW_C
    return pl.pallas_call(
        functools.partial(_rw_feat_kernel, tm=tm, n_ctx=n_ctx, n_tot=n),
        grid=(bsz, n // tm),
        in_specs=[pl.BlockSpec((1, tm, wz), lambda b, j: (b, j, 0)),
                  pl.BlockSpec((1, 8, wz), lambda b, j: (b, jnp.maximum(j * (tm // 8) - 1, 0), 0)),
                  pl.BlockSpec((1, 8, wz), lambda b, j: (b, jnp.minimum((j + 1) * (tm // 8), nb8 - 1), 0)),
                  full((1, wz)), full((1, wz)), full((lw, 2 * w)), full((lw, 2 * w)),
                  full((1, 2 * w)), full((1, 2 * w)), full((1, w)), full((1, w)), full((1, w)), full((w, w))],
        out_specs=[pl.BlockSpec((1, 2, RW_FEATS, RW_HEADS, tm, RW_N), lambda b, j: (b, 0, 0, 0, j, 0)),
                   pl.BlockSpec((1, RW_HEADS, tm, RW_N), lambda b, j: (b, 0, j, 0)),
                   pl.BlockSpec((1, 2, 1, gpt, w), lambda b, j: (b, 0, j, 0, 0)),
                   pl.BlockSpec((1, tm, w), lambda b, j: (b, j, 0))],
        out_shape=[jax.ShapeDtypeStruct((bsz, 2, RW_FEATS, RW_HEADS, n, RW_N), F32),
                   jax.ShapeDtypeStruct((bsz, RW_HEADS, n, RW_N), F32),
                   jax.ShapeDtypeStruct((bsz, 2, n // tm, gpt, w), F32),
                   jax.ShapeDtypeStruct((bsz, n, w), F32)],
        compiler_params=_cparams(("parallel", "parallel"), 56),
        name="rwkv_features",
    )(rw, rw, rw, vec(mu_prev), vec(mu_next), w2p, a2p, vec(w0), vec(a0), vec(k_k), vec(k_a), vec(r_k), seg_j)


def _tri_inverse(a_list, ri, ci):
    mm = lambda ps, qs: [_mx(p, q, NN, RW_PASSES_INV) for p, q in zip(ps, qs)]
    eye = jnp.where(ri == ci, 1.0, 0.0)
    x = [jnp.where((ri // 8) == (ci // 8), -a, 0.0) for a in a_list]
    p = [eye + t for t in x]
    x2 = mm(x, x)
    p = [s + t for s, t in zip(p, mm(p, x2))]
    x4 = mm(x2, x2)
    p = [s + t for s, t in zip(p, mm(p, x4))]
    blk = 8
    while blk < RW_C:
        off = ((ri // (2 * blk)) == (ci // (2 * blk))) & ((ri // blk) != (ci // blk))
        e = [jnp.where(off, a, 0.0) for a in a_list]
        p = [s - t for s, t in zip(p, mm(mm(p, e), p))]
        blk *= 2
    return p


def _rw_chunk_kernel(f_ref, v_ref, gc_ref, mn_out, q_out, o_out):
    c = pl.program_id(1)
    gpt = gc_ref.shape[3]
    nk = RW_N
    ri = lax.broadcasted_iota(jnp.int32, (RW_C, RW_C), 0)
    ci = lax.broadcasted_iota(jnp.int32, (RW_C, RW_C), 1)
    eye = ri == ci
    before = ((ci < ri), (ci > ri))
    upto = tuple(m | eye for m in before)
    g_all = [gc_ref[0, d, 0, pl.ds(c % gpt, 1), :] for d in range(2)]
    ent = [(d, h) for d in range(2) for h in range(RW_HEADS)]
    idx = range(len(ent))
    kap, bet, kt, rt, kh, bh = ([f_ref[0, d, q, h] for d, h in ent] for q in range(RW_FEATS))
    v = [v_ref[0, h] for _, h in ent]
    kr = [jnp.concatenate([kap[i], rt[i]], axis=0) for i in idx]
    sb = [_mx(kr[i], bet[i], NT, RW_PASSES_SCORE) for i in idx]
    sk = [_mx(kr[i], kt[i], NT, RW_PASSES_SCORE) for i in idx]
    a_ab = [jnp.where(before[ent[i][0]], sb[i][0:RW_C], 0.0) for i in idx]
    a_qb = [jnp.where(upto[ent[i][0]], sb[i][RW_C:], 0.0) for i in idx]
    a_k = [jnp.concatenate([jnp.where(before[ent[i][0]], sk[i][0:RW_C], 0.0),
                            jnp.where(upto[ent[i][0]], sk[i][RW_C:], 0.0)], axis=0) for i in idx]
    akv = [_mx(a_k[i], v[i], NN, RW_PASSES_MIX) for i in idx]
    vk = [_mx(v[i], kh[i], TN, RW_PASSES_MIX) for i in idx]
    t_inv = _tri_inverse(a_ab, ri, ci)
    wu = [_mx(t_inv[i], jnp.concatenate([kap[i], akv[i][0:RW_C]], axis=1), NN, RW_PASSES_MIX)
          for i in idx]
    wub = [_mx(wu[i], bh[i], TN, RW_PASSES_MIX) for i in idx]
    aq = [_mx(a_qb[i], wu[i], NN, RW_PASSES_MIX) for i in idx]
    for i, (d, h) in enumerate(ent):
        g = g_all[d][:, h * nk:(h + 1) * nk]
        mn_out[0, d, 0, 0, h] = jnp.where(eye, g, 0.0) - wub[i][0:nk]
        mn_out[0, d, 0, 1, h] = vk[i] - wub[i][nk:]
        q_out[0, d, 0, h] = (rt[i] - aq[i][:, 0:nk]).astype(q_out.dtype)
        o_out[0, d, 0, h] = akv[i][RW_C:] - aq[i][:, nk:]


def _rw_chunks(feats, v, gc):
    bsz, _, _, _, n, _ = feats.shape
    nch = n // RW_C
    gpt = gc.shape[3]
    q_dtype = BF16 if RW_PASSES_OUT == 1 else F32
    return pl.pallas_call(
        _rw_chunk_kernel,
        grid=(bsz, nch),
        in_specs=[pl.BlockSpec((1, 2, RW_FEATS, RW_HEADS, RW_C, RW_N), lambda b, c: (b, 0, 0, 0, c, 0)),
                  pl.BlockSpec((1, RW_HEADS, RW_C, RW_N), lambda b, c: (b, 0, c, 0)),
                  pl.BlockSpec((1, 2, 1, gpt, BRANCH_W), lambda b, c: (b, 0, c // gpt, 0, 0))],
        out_specs=[pl.BlockSpec((1, 2, 1, 2, RW_HEADS, RW_N, RW_N), lambda b, c: (b, 0, c, 0, 0, 0, 0)),
                   pl.BlockSpec((1, 2, 1, RW_HEADS, RW_C, RW_N), lambda b, c: (b, 0, c, 0, 0, 0)),
                   pl.BlockSpec((1, 2, 1, RW_HEADS, RW_C, RW_N), lambda b, c: (b, 0, c, 0, 0, 0))],
        out_shape=[jax.ShapeDtypeStruct((bsz, 2, nch, 2, RW_HEADS, RW_N, RW_N), F32),
                   jax.ShapeDtypeStruct((bsz, 2, nch, RW_HEADS, RW_C, RW_N), q_dtype),
                   jax.ShapeDtypeStruct((bsz, 2, nch, RW_HEADS, RW_C, RW_N), F32)],
        compiler_params=_cparams(("parallel", "parallel"), 32),
        name="rwkv_chunks",
    )(feats, v, gc)


def _rw_scan_kernel(mnf_ref, mnb_ref, qf_ref, qb_ref, of_ref, ob_ref, yf_ref, yb_ref, state):
    @pl.when(pl.program_id(1) == 0)
    def _():
        state[...] = jnp.zeros_like(state)

    for d, (mn_ref, q_ref, o_ref, y_ref) in enumerate(((mnf_ref, qf_ref, of_ref, yf_ref),
                                                       (mnb_ref, qb_ref, ob_ref, yb_ref))):
        outs = []
        for h in range(RW_HEADS):
            s0 = state[d, h]
            outs.append(_mx(q_ref[0, 0, 0, h], s0, NT, RW_PASSES_OUT) + o_ref[0, 0, 0, h])
            state[d, h] = _mx(s0, mn_ref[0, 0, 0, 0, h], NN, RW_PASSES_STATE) + mn_ref[0, 0, 0, 1, h]
        y_ref[0] = jnp.concatenate(outs, axis=1)


def _rw_scan(mn, q, o, n_ctx):
    bsz, _, nch, _, _, _, _ = mn.shape
    nch_ctx = n_ctx // RW_C
    n = nch * RW_C

    def rv(s):
        return jnp.where(s < nch_ctx, nch_ctx - 1 - s, nch + nch_ctx - 1 - s)

    mn_spec = lambda d, f: pl.BlockSpec((1, 1, 1, 2, RW_HEADS, RW_N, RW_N), lambda b, s: (b, d, f(s), 0, 0, 0, 0))
    qo_spec = lambda d, f: pl.BlockSpec((1, 1, 1, RW_HEADS, RW_C, RW_N), lambda b, s: (b, d, f(s), 0, 0, 0))
    ident = lambda s: s
    return pl.pallas_call(
        _rw_scan_kernel,
        grid=(bsz, nch),
        in_specs=[mn_spec(0, ident), mn_spec(1, rv), qo_spec(0, ident), qo_spec(1, rv), qo_spec(0, ident),
                  qo_spec(1, rv)],
        out_specs=[pl.BlockSpec((1, RW_C, BRANCH_W), lambda b, s: (b, s, 0)),
                   pl.BlockSpec((1, RW_C, BRANCH_W), lambda b, s: (b, rv(s), 0))],
        out_shape=[jax.ShapeDtypeStruct((bsz, n, BRANCH_W), F32), jax.ShapeDtypeStruct((bsz, n, BRANCH_W), F32)],
        scratch_shapes=[pltpu.VMEM((2, RW_HEADS, RW_N, RW_N), F32)],
        compiler_params=_cparams(("parallel", "arbitrary"), 32),
        name="rwkv_scan",
    )(mn, mn, q, q, o, o)


def _conv3_kernel(x_ref, p_ref, n_ref, w_ref, b_ref, o_ref, *, tm, n_blocks):
    j = pl.program_id(1)
    z = x_ref[0]
    rowi = lax.broadcasted_iota(jnp.int32, (tm, 1), 0)
    prev, nxt = _shift_rows(z, p_ref[0, 7:8, :], n_ref[0, 0:1, :],
                            (rowi == 0) & (j == 0), (rowi == tm - 1) & (j == n_blocks - 1))
    o_ref[0] = prev * w_ref[0:1, :] + z * w_ref[1:2, :] + nxt * w_ref[2:3, :] + b_ref[...]


def _hy_conv3(hy, conv_w, conv_b, seg_start, seg_len):
    bsz, _, w = hy.shape
    tm = ROW_TILE
    off, nb = seg_start // tm, seg_len // tm
    off8, nb8 = seg_start // 8, seg_len // 8
    return pl.pallas_call(
        functools.partial(_conv3_kernel, tm=tm, n_blocks=nb),
        grid=(bsz, nb),
        in_specs=[pl.BlockSpec((1, tm, w), lambda b, j: (b, off + j, 0)),
                  pl.BlockSpec((1, 8, w), lambda b, j: (b, off8 + jnp.maximum(j * (tm // 8) - 1, 0), 0)),
                  pl.BlockSpec((1, 8, w), lambda b, j: (b, off8 + jnp.minimum((j + 1) * (tm // 8), nb8 - 1), 0)),
                  pl.BlockSpec((3, w), lambda b, j: (0, 0)),
                  pl.BlockSpec((1, w), lambda b, j: (0, 0))],
        out_specs=pl.BlockSpec((1, tm, w), lambda b, j: (b, j, 0)),
        out_shape=jax.ShapeDtypeStruct((bsz, seg_len, w), F32),
        compiler_params=_cparams(("parallel", "parallel"), 32),
        name="hyena_conv3",
    )(hy, hy, hy, conv_w, conv_b.reshape(1, w))


def _hy_embedding(n):
    t = jnp.linspace(0.0, 1.0, n, dtype=F32)[:, None]
    ang = 2.0 * math.pi * jnp.arange(n, dtype=F32)[:, None] / n
    bands = jnp.linspace(1e-4, HY_BANDS - 1, HY_BANDS, dtype=F32)[None, :]
    z = jnp.concatenate([t, jnp.cos(bands * ang), -jnp.sin(bands * ang)], axis=-1)
    return jnp.pad(z, ((0, 0), (0, HY_FFN - HY_EMB)))


def _hy_filter_kernel(e_ref, w1_ref, b1_ref, f1_ref, w2_ref, b2_ref, f2_ref, w3_ref, dl_ref, o_ref):
    e = e_ref[...]
    h = jnp.sin(f1_ref[...] * (_dot(e, w1_ref[...], HI) + b1_ref[...]))
    h = jnp.sin(f2_ref[...] * (_dot(h, w2_ref[...], HI) + b2_ref[...]))
    o_ref[...] = _dot(h, w3_ref[...], HI) * jnp.exp(-e[:, 0:1] * dl_ref[...])


def _hy_filters(n, w1, b1, f1, w2, b2, f2, w3):
    tm = min(ROW_TILE, n)
    wf = w3.shape[1]
    emb = _hy_embedding(n)
    w1p = jnp.pad(w1, ((0, HY_FFN - HY_EMB), (0, 0)))
    deltas = jnp.abs(jnp.linspace(math.log(HY_TARGET) / HY_SLOW_PCT, math.log(HY_TARGET) / HY_FAST_PCT,
                                  BRANCH_W, dtype=F32))
    dl = jnp.tile(deltas, wf // BRANCH_W).reshape(1, wf)
    vec = lambda t: t.reshape(1, -1)
    full = lambda shape: pl.BlockSpec(shape, lambda i: (0,) * len(shape))
    return pl.pallas_call(
        _hy_filter_kernel,
        grid=(n // tm,),
        in_specs=[pl.BlockSpec((tm, HY_FFN), lambda i: (i, 0)),
                  full((HY_FFN, HY_FFN)), full((1, HY_FFN)), full((1, HY_FFN)),
                  full((HY_FFN, HY_FFN)), full((1, HY_FFN)), full((1, HY_FFN)),
                  full((HY_FFN, wf)), full((1, wf))],
        out_specs=pl.BlockSpec((tm, wf), lambda i: (i, 0)),
        out_shape=jax.ShapeDtypeStruct((n, wf), F32),
        compiler_params=_cparams(("parallel",), 32),
        name="hyena_filters",
    )(emb, w1p, vec(b1), vec(f1), w2, vec(b2), vec(f2), w3, dl)


def _dft_mats(n):
    k = jnp.arange(n, dtype=jnp.int32)
    ang = (math.pi / n) * ((k[:, None] * k[None, :]) % (2 * n)).astype(F32)
    alt = jnp.where(k % 2 == 0, 1.0, -1.0).astype(F32)
    cos, sin = jnp.cos(ang), jnp.sin(ang)
    fwd = jnp.concatenate([cos, (-sin).at[0].set(alt)], axis=0)
    wk = jnp.where(k == 0, 1.0, 2.0).astype(F32)[None, :]
    inv = jnp.concatenate([wk * cos, (-2.0 * sin).at[:, 0].set(alt)], axis=1) / (2 * n)
    return fwd.astype(BF16), inv.astype(BF16)


def _hy_spec_kernel(fc_ref, fs_ref, h_ref, g_ref, *, tk):
    h = h_ref[...]
    n, w2 = h.shape
    w = w2 // 2
    rowi = lax.broadcasted_iota(jnp.int32, (n, w2), 0)
    coli = lax.broadcasted_iota(jnp.int32, (n, w2), 1)
    h = jnp.where((rowi == 0) & (coli >= w), 0.0, h)
    hi = h.astype(BF16)
    lo = (h - hi.astype(F32)).astype(BF16)
    sr = _dot(fc_ref[...], hi) + _dot(fc_ref[...], lo)
    si = _dot(fs_ref[...], hi) + _dot(fs_ref[...], lo)
    krow = pl.program_id(1) * tk + lax.broadcasted_iota(jnp.int32, (tk, 1), 0)
    g_ref[0, 0] = sr[:, 0:w] + sr[:, w:w2]
    g_ref[0, 1] = si[:, 0:w] + jnp.where(krow == 0, 1.0, -1.0) * si[:, w:w2]


def _hy_spectrum(filt, fwd, n):
    w = BRANCH_W
    tk = min(ROW_TILE, n)
    nk = n // tk
    return pl.pallas_call(
        functools.partial(_hy_spec_kernel, tk=tk),
        grid=(2, nk),
        in_specs=[pl.BlockSpec((tk, n), lambda j, kt: (kt, 0)),
                  pl.BlockSpec((tk, n), lambda j, kt: (nk + kt, 0)),
                  pl.BlockSpec((n, 2 * w), lambda j, kt: (0, j))],
        out_specs=pl.BlockSpec((1, 2, tk, w), lambda j, kt: (j, 0, kt, 0)),
        out_shape=jax.ShapeDtypeStruct((2, 2, n, w), F32),
        compiler_params=_cparams(("parallel", "parallel"), 48),
        name="hyena_filter_spectrum",
    )(fwd, fwd, filt)


HY_BATCH = 2


def _hy_fwd_kernel(fc_ref, fs_ref, u_ref, g_ref, z_ref):
    w = BRANCH_W
    u = jnp.concatenate([u_ref[i] for i in range(HY_BATCH)], axis=1).astype(BF16)
    ur = _dot(fc_ref[...], u)
    ui = _dot(fs_ref[...], u)
    gr = jnp.concatenate([g_ref[0]] * HY_BATCH, axis=1)
    gi = jnp.concatenate([g_ref[1]] * HY_BATCH, axis=1)
    tk = ur.shape[0]
    first = (pl.program_id(1) == 0) & (lax.broadcasted_iota(jnp.int32, (tk, 1), 0) == 0)
    zr = jnp.where(first, ur * gr, ur * gr - ui * gi).astype(BF16)
    zi = jnp.where(first, ui * gi, ur * gi + ui * gr).astype(BF16)
    for i in range(HY_BATCH):
        z_ref[i, 0] = zr[:, i * w:(i + 1) * w]
        z_ref[i, 1] = zi[:, i * w:(i + 1) * w]


def _hy_inv_kernel(gi_ref, z_ref, u_ref, x_ref, skip_ref, o_ref):
    w = BRANCH_W
    y = _dot(gi_ref[...], jnp.concatenate([z_ref[i] for i in range(HY_BATCH)], axis=1))
    for i in range(HY_BATCH):
        o_ref[i] = x_ref[i] * (y[:, i * w:(i + 1) * w] + skip_ref[...] * u_ref[i])


def _hy_long_conv(src, src_col, gate_src, gate_col, spec_j, skip_j, fwd, inv):
    bsz, n, _ = src.shape
    w = BRANCH_W
    hb = HY_BATCH
    assert bsz % hb == 0
    tk = min(ROW_TILE, n)
    nk = n // tk
    z = pl.pallas_call(
        _hy_fwd_kernel,
        grid=(bsz // hb, nk),
        in_specs=[pl.BlockSpec((tk, n), lambda b, kt: (kt, 0)),
                  pl.BlockSpec((tk, n), lambda b, kt: (nk + kt, 0)),
                  pl.BlockSpec((hb, n, w), lambda b, kt: (b, 0, src_col)),
                  pl.BlockSpec((2, tk, w), lambda b, kt: (0, kt, 0))],
        out_specs=pl.BlockSpec((hb, 2, tk, w), lambda b, kt: (b, 0, kt, 0)),
        out_shape=jax.ShapeDtypeStruct((bsz, 2, n, w), BF16),
        compiler_params=_cparams(("parallel", "parallel"), 56),
        name="hyena_dft",
    )(fwd, fwd, src, spec_j)
    z = z.reshape(bsz, 2 * n, w)
    return pl.pallas_call(
        _hy_inv_kernel,
        grid=(bsz // hb, nk),
        in_specs=[pl.BlockSpec((tk, 2 * n), lambda b, t: (t, 0)),
                  pl.BlockSpec((hb, 2 * n, w), lambda b, t: (b, 0, 0)),
                  pl.BlockSpec((hb, tk, w), lambda b, t: (b, t, src_col)),
                  pl.BlockSpec((hb, tk, w), lambda b, t: (b, t, gate_col)),
                  pl.BlockSpec((1, w), lambda b, t: (0, 0))],
        out_specs=pl.BlockSpec((hb, tk, w), lambda b, t: (b, t, 0)),
        out_shape=jax.ShapeDtypeStruct((bsz, n, w), F32),
        compiler_params=_cparams(("parallel", "parallel"), 56),
        name="hyena_idft",
    )(inv, z, src, gate_src, skip_j.reshape(1, w))


def _hyena_segment(hy, seg_start, seg_len, conv_w, conv_b, ffn, skip):
    z = _hy_conv3(hy, conv_w, conv_b, seg_start, seg_len)
    filt = _hy_filters(seg_len, *ffn)
    fwd, inv = _dft_mats(seg_len)
    spec = _hy_spectrum(filt, fwd, seg_len)
    y1 = _hy_long_conv(z, 0, z, 1, spec[0], skip[0], fwd, inv)
    return _hy_long_conv(y1, 0, z, 2, spec[1], skip[1], fwd, inv)


def _merge_kernel(x_ref, gx_ref, gc_ref, s5u_ref, s5y_ref, d_ref, gw_ref, gb_ref, att_ref,
                  rwf_ref, rwb_ref, bv_ref, lng_ref, lnb_ref, hyl_ref, hyc_ref, gate_ref, bg_ref, wo_ref, j_ref,
                  o_ref, *, n_ctx_blocks, blk_off):
    w = BRANCH_W
    is_ctx = (pl.program_id(1) + blk_off) < n_ctx_blocks

    def rms(y, g):
        return y * lax.rsqrt(jnp.mean(y * y, axis=-1, keepdims=True) + NORM_EPS) * g

    ys = d_ref[...] * s5u_ref[0] + s5y_ref[0]
    ys = 0.5 * ys * (1.0 + lax.erf(ys * (2.0 ** -0.5)))
    ys = ys * jax.nn.sigmoid(_dot(ys.astype(BF16), gw_ref[...]) + gb_ref[...])
    ys = rms(ys, bg_ref[0:1, :])
    ya = rms(att_ref[0], bg_ref[1:2, :])
    seg_j = j_ref[...]
    yr = rwf_ref[0] + rwb_ref[0]
    mu = _segsum(yr, seg_j) * (1.0 / RW_N)
    yc = yr - mu
    var = _segsum(yc * yc, seg_j) * (1.0 / RW_N)
    yr = yc * lax.rsqrt(var + RW_LN_EPS) * lng_ref[...] + lnb_ref[...] + bv_ref[0]
    yh = rms(jnp.where(is_ctx, hyc_ref[0], hyl_ref[0]), bg_ref[2:3, :])
    gp = gate_ref[0]
    sg = gp * jax.nn.sigmoid(gp)
    acc = _dot((ys * sg[:, 0:w]).astype(BF16), wo_ref[0:w, :])
    acc += _dot((ya * sg[:, w:2 * w]).astype(BF16), wo_ref[w:2 * w, :])
    acc += _dot((yr * sg[:, 2 * w:3 * w]).astype(BF16), wo_ref[2 * w:3 * w, :])
    acc += _dot((yh * sg[:, 3 * w:4 * w]).astype(BF16), wo_ref[3 * w:4 * w, :])
    gate = jnp.where(is_ctx, gc_ref[0, 0], gx_ref[0, 0])
    o_ref[0] = x_ref[0] + gate * acc


def _merge(x_all, mod_l, s5u, s5y, d_skip, glu_w, glu_b, att, rwf, rwb, bv, ln_g, ln_b, hy_lat, hy_ctx,
           gate_pre, branch_g, w_out, seg_j, n_ctx, with_ctx):
    bsz, n, d = x_all.shape
    w = BRANCH_W
    tm = ROW_TILE
    ncb = n_ctx // tm
    off = 0 if with_ctx else ncb
    nb = n // tm - off
    nlb = (n - n_ctx) // tm
    ctx_row = bsz
    row = lambda width: pl.BlockSpec((1, tm, width), lambda b, j: (b, j + off, 0))
    full = lambda shape: pl.BlockSpec(shape, lambda b, j: (0,) * len(shape))
    if hy_ctx is None:
        hy_ctx = hy_lat
        hyc_spec = pl.BlockSpec((1, tm, w), lambda b, j: (b, 0, 0))
    else:
        hyc_spec = pl.BlockSpec((1, tm, w), lambda b, j: (b, jnp.minimum(j + off, ncb - 1), 0))
    vec = lambda t: t.reshape(1, -1)
    return pl.pallas_call(
        functools.partial(_merge_kernel, n_ctx_blocks=ncb, blk_off=off),
        grid=(bsz, nb),
        in_specs=[row(d),
                  pl.BlockSpec((1, 1, 1, d), lambda b, j: (b, 2, 0, 0)),
                  pl.BlockSpec((1, 1, 1, d), lambda b, j: (ctx_row, 2, 0, 0)),
                  row(w), row(w), full((1, w)), full((w, w)), full((1, w)),
                  row(w), row(w), row(w), row(w), full((1, w)), full((1, w)),
                  pl.BlockSpec((1, tm, w), lambda b, j: (b, jnp.clip(j + off - ncb, 0, nlb - 1), 0)),
                  hyc_spec, row(4 * w), full((3, w)), full((4 * w, d)), full((w, w))],
        out_specs=pl.BlockSpec((1, tm, d), lambda b, j: (b, j, 0)),
        out_shape=jax.ShapeDtypeStruct((bsz, nb * tm, d), F32),
        compiler_params=_cparams(("parallel", "parallel"), 56),
        name="merge_out_proj",
    )(x_all, mod_l, mod_l, s5u, s5y, vec(d_skip), glu_w.astype(BF16), vec(glu_b), att,
      rwf, rwb, bv, vec(ln_g), vec(ln_b), hy_lat, hy_ctx, gate_pre, branch_g, w_out, seg_j)


def kernel(x, c, ctx, c_ctx, norm_g, w_ada, b_ada, w_in, w_out, branch_g, s5_lam_re, s5_lam_im, s5_log_step, s5_b_re, s5_b_im, s5_c_re, s5_c_im, s5_d, s5_glu_w, s5_glu_b, att_q_g, att_k_g, att_sink, rw_mu_prev, rw_mu_next, rw_w0, rw_w2, rw_a0, rw_a2, rw_k_k, rw_k_a, rw_r_k, rw_ln_g, rw_ln_b, hy_conv_w, hy_conv_b, hy_w1, hy_b1, hy_f1, hy_w2, hy_b2, hy_f2, hy_w3, hy_skip):
    bsz, n_lat, d = x.shape
    n_ctx = ctx.shape[1]
    depth = w_ada.shape[0]
    n = n_ctx + n_lat
    assert bsz + 1 <= MOD_ROWS and n_ctx % ROW_TILE == 0 and n_lat % ROW_TILE == 0

    c_all = jnp.zeros((MOD_ROWS, d), F32).at[0:bsz].set(c).at[bsz].set(c_ctx)
    mod = _ada_mod(c_all, w_ada, b_ada).reshape(depth, MOD_ROWS, 3, 1, d)
    w_in_b = w_in.astype(BF16)
    w_out_b = w_out.astype(BF16)
    rope_cos, rope_sin = _rope_tables(n_ctx, n_lat)
    lane = jnp.arange(BRANCH_W)
    seg_j = (lane[:, None] // HEAD_DIM == lane[None, :] // HEAD_DIM).astype(BF16)
    n_scan_steps = max(1, math.ceil(math.log2(n // S5_T)))
    slabs = (P_S5, P_QKV, P_RW, P_HY, P_GATE)
    tiles = (P_S5, P_QKV, P_RW, P_HY // 2, P_GATE // 2)
    offs = [sum(slabs[:i]) for i in range(len(slabs))]

    x_all = jnp.concatenate([ctx, x], axis=1)
    for l in range(depth):
        with_ctx = l < depth - 1
        h = _norm_mod(x_all, norm_g[l], mod[l], n_ctx).reshape(bsz * n, d)
        s5u, qkv, rw, hy, gate_pre = (
            _mm(h, w_in_b[l][:, o:o + wd], 512, tn).reshape(bsz, n, wd) for o, wd, tn in zip(offs, slabs, tiles))

        s5_w = _s5_weights(s5_lam_re[l], s5_lam_im[l], s5_log_step[l], s5_b_re[l], s5_b_im[l],
                           s5_c_re[l], s5_c_im[l], n_scan_steps, n_lat // S5_T, n_ctx // S5_T)
        s5y = _s5_mix(s5u, s5_w, n_ctx)

        qh, kh, vh = _qk_prep(qkv, rope_cos, rope_sin, att_q_g[l], att_k_g[l], seg_j)
        att = _attn(qh, kh, vh, att_sink[l], n_ctx)

        feats, rv, gc, bv = _rw_features(
            rw, (rw_mu_prev[l], rw_mu_next[l], rw_w0[l], rw_w2[l], rw_a0[l], rw_a2[l], rw_k_k[l], rw_k_a[l],
                 rw_r_k[l].reshape(-1)), seg_j, n_ctx)
        mn, rq, ro = _rw_chunks(feats, rv, gc)
        rwf, rwb = _rw_scan(mn, rq, ro, n_ctx)

        ffn = (hy_w1[l], hy_b1[l], hy_f1[l], hy_w2[l], hy_b2[l], hy_f2[l], hy_w3[l])
        hy_lat = _hyena_segment(hy, n_ctx, n_lat, hy_conv_w[l], hy_conv_b[l], ffn, hy_skip[l])
        hy_ctx = _hyena_segment(hy, 0, n_ctx, hy_conv_w[l], hy_conv_b[l], ffn, hy_skip[l]) if with_ctx else None

        x_all = _merge(x_all, mod[l], s5u, s5y, s5_d[l].reshape(-1), s5_glu_w[l], s5_glu_b[l], att,
                       rwf, rwb, bv, rw_ln_g[l], rw_ln_b[l], hy_lat, hy_ctx, gate_pre, branch_g[l], w_out_b[l],
                       seg_j, n_ctx, with_ctx)
    return x_all
```

```python
import functools
import math

import jax
import jax.numpy as jnp
from jax import lax
from jax.experimental import pallas as pl
from jax.experimental.pallas import tpu as pltpu

F32 = jnp.float32
BF16 = jnp.bfloat16
HI = lax.Precision.HIGHEST

D_MODEL = 2048
GRID_W = 64
BRANCH_W = 512
NORM_EPS = 1e-6
S5_H = 16
S5_G = BRANCH_W // S5_H
S5_P = 64
HEAD_DIM = 64
ATT_HEADS = 8
ATT_KV_HEADS = 2
ATT_REP = ATT_HEADS // ATT_KV_HEADS
WINDOW = 128
BLOCK = 128
ROPE_BASE = 10000.0
NEG_INF = -1e30
RW_N = 64
RW_HEADS = 8
RW_LORA = 32
RW_LN_EPS = 64e-5
HY_EMB = 33
HY_BANDS = 16
HY_FFN = 64
HY_TARGET = 1e-2
HY_FAST_PCT = 0.3
HY_SLOW_PCT = 1.5
P_S5 = BRANCH_W
P_QKV = ATT_HEADS * HEAD_DIM + 2 * ATT_KV_HEADS * HEAD_DIM
P_RW = 3 * BRANCH_W + 4 * RW_LORA
P_HY = 3 * BRANCH_W
P_GATE = 4 * BRANCH_W

V7X_VMEM_BYTES = 64 * 1024 * 1024
S5_T = 16
RW_C = 64
ROW_TILE = 256
MOD_ROWS = 24


def _cparams(sem, vmem_mb):
    assert vmem_mb * 1024 * 1024 <= V7X_VMEM_BYTES
    return pltpu.CompilerParams(dimension_semantics=sem, vmem_limit_bytes=vmem_mb * 1024 * 1024)


def _dot(a, b, precision=None):
    return jnp.dot(a, b, precision=precision, preferred_element_type=F32)


def _dot_nt(a, b, precision=None):
    return lax.dot_general(a, b, (((1,), (1,)), ((), ())), precision=precision, preferred_element_type=F32)


def _dot_tn(a, b, precision=None):
    return lax.dot_general(a, b, (((0,), (0,)), ((), ())), precision=precision, preferred_element_type=F32)


NN = ((1,), (0,))
NT = ((1,), (1,))
TN = ((0,), (0,))


def _split(x):
    hi = x.astype(BF16)
    return hi, (x - hi.astype(F32)).astype(BF16)


def _mx(a, b, dims, passes):
    dg = lambda p, q: lax.dot_general(p, q, (dims, ((), ())), preferred_element_type=F32)
    if passes == 1:
        return dg(a.astype(BF16), b.astype(BF16))
    ah, al = _split(a)
    bh, bl = _split(b)
    return dg(ah, bh) + dg(ah, bl) + dg(al, bh)


def _mx_exact_lhs(a_bf16, b):
    bh, bl = _split(b)
    return _dot(a_bf16, bh) + _dot(a_bf16, bl)


def _segsum(x, j):
    hi, lo = _split(x)
    return _dot(hi, j) + _dot(lo, j)


def _shift_rows(z, prev_row, next_row, zero_prev, zero_next):
    tm = z.shape[0]
    rowi = lax.broadcasted_iota(jnp.int32, (tm, 1), 0)
    prev = jnp.where(rowi == 0, prev_row, pltpu.roll(z, 1, 0))
    nxt = jnp.where(rowi == tm - 1, next_row, pltpu.roll(z, tm - 1, 0))
    prev = jnp.where(zero_prev, 0.0, prev)
    nxt = jnp.where(zero_next, 0.0, nxt)
    return prev, nxt


def _mod_kernel(c_ref, w_ref, b_ref, o_ref):
    c = c_ref[...]
    s = c * jax.nn.sigmoid(c)
    o_ref[0] = _dot(s, w_ref[0], HI) + b_ref[0]


def _ada_mod(c_all, w_ada, b_ada):
    depth, d, d3 = w_ada.shape
    tn = 512
    return pl.pallas_call(
        _mod_kernel,
        grid=(depth, d3 // tn),
        in_specs=[pl.BlockSpec((MOD_ROWS, d), lambda l, j: (0, 0)),
                  pl.BlockSpec((1, d, tn), lambda l, j: (l, 0, j)),
                  pl.BlockSpec((1, 1, tn), lambda l, j: (l, 0, j))],
        out_specs=pl.BlockSpec((1, MOD_ROWS, tn), lambda l, j: (l, 0, j)),
        out_shape=jax.ShapeDtypeStruct((depth, MOD_ROWS, d3), F32),
        compiler_params=_cparams(("parallel", "parallel"), 32),
        name="ada_mod",
    )(c_all, w_ada, b_ada.reshape(depth, 1, d3))


def _norm_mod_kernel(x_ref, g_ref, shx_ref, scx_ref, shc_ref, scc_ref, o_ref, *, n_ctx_blocks):
    x = x_ref[0]
    ms = jnp.mean(x * x, axis=-1, keepdims=True)
    y = x * lax.rsqrt(ms + NORM_EPS) * g_ref[...]
    is_ctx = pl.program_id(1) < n_ctx_blocks
    sc = jnp.where(is_ctx, scc_ref[0, 0], scx_ref[0, 0])
    sh = jnp.where(is_ctx, shc_ref[0, 0], shx_ref[0, 0])
    o_ref[0] = (y * (1.0 + sc) + sh).astype(BF16)


def _norm_mod(x_all, g, mod_l, n_ctx):
    bsz, n, d = x_all.shape
    tm = ROW_TILE
    ctx_row = bsz
    return pl.pallas_call(
        functools.partial(_norm_mod_kernel, n_ctx_blocks=n_ctx // tm),
        grid=(bsz, n // tm),
        in_specs=[pl.BlockSpec((1, tm, d), lambda b, j: (b, j, 0)),
                  pl.BlockSpec((1, d), lambda b, j: (0, 0)),
                  pl.BlockSpec((1, 1, 1, d), lambda b, j: (b, 0, 0, 0)),
                  pl.BlockSpec((1, 1, 1, d), lambda b, j: (b, 1, 0, 0)),
                  pl.BlockSpec((1, 1, 1, d), lambda b, j: (ctx_row, 0, 0, 0)),
                  pl.BlockSpec((1, 1, 1, d), lambda b, j: (ctx_row, 1, 0, 0))],
        out_specs=pl.BlockSpec((1, tm, d), lambda b, j: (b, j, 0)),
        out_shape=jax.ShapeDtypeStruct((bsz, n, d), BF16),
        compiler_params=_cparams(("parallel", "parallel"), 32),
        name="norm_mod",
    )(x_all, g.reshape(1, d), mod_l, mod_l, mod_l, mod_l)


def _mm_kernel(a_ref, b_ref, o_ref):
    o_ref[...] = _dot(a_ref[...], b_ref[...]).astype(o_ref.dtype)


def _mm(a, b, tm, tn):
    m, k = a.shape
    n = b.shape[1]
    assert m % tm == 0 and n % tn == 0
    return pl.pallas_call(
        _mm_kernel,
        grid=(n // tn, m // tm),
        in_specs=[pl.BlockSpec((tm, k), lambda j, i: (i, 0)),
                  pl.BlockSpec((k, tn), lambda j, i: (0, j))],
        out_specs=pl.BlockSpec((tm, tn), lambda j, i: (i, j)),
        out_shape=jax.ShapeDtypeStruct((m, n), F32),
        compiler_params=_cparams(("parallel", "parallel"), 48),
        name="in_proj",
    )(a, b)


def _s5_weights(lam_re, lam_im, log_step, b_re, b_im, c_re, c_im, n_scan_steps, nc_lat, nc_ctx):
    t_len = S5_T
    step = jnp.exp(log_step)[..., None]
    th_re, th_im = lam_re * step, lam_im * step

    def cpow(k):
        mag = jnp.exp(th_re[..., None] * k)
        ang = th_im[..., None] * k
        return mag * jnp.cos(ang), mag * jnp.sin(ang)

    lb_re, lb_im = (t[..., 0] for t in cpow(jnp.ones((1,), F32)))
    den = lam_re * lam_re + lam_im * lam_im
    nr = lb_re - 1.0
    co_re = (nr * lam_re + lb_im * lam_im) / den
    co_im = (lb_im * lam_re - nr * lam_im) / den
    bb_re = co_re[..., None] * b_re - co_im[..., None] * b_im
    bb_im = co_re[..., None] * b_im + co_im[..., None] * b_re

    lags = jnp.arange(t_len + 1, dtype=F32)
    pw_re, pw_im = cpow(lags)
    x_re = c_re[..., None] * pw_re[:, :, None] - c_im[..., None] * pw_im[:, :, None]
    x_im = c_re[..., None] * pw_im[:, :, None] + c_im[..., None] * pw_re[:, :, None]
    m_k = (jnp.einsum('dgopk,dgpi->dgkio', x_re, bb_re, precision=HI)
           - jnp.einsum('dgopk,dgpi->dgkio', x_im, bb_im, precision=HI))
    s_idx = jnp.arange(t_len)[:, None]
    t_idx = jnp.arange(t_len)[None, :]
    lag_f = t_idx - s_idx
    kin = []
    for d, lag in enumerate((lag_f, -lag_f)):
        blk = m_k[d][:, jnp.clip(lag, 0, t_len)]
        blk = jnp.where((lag >= 0)[None, :, :, None, None], blk, 0.0)
        kin.append(blk.transpose(0, 1, 3, 2, 4).reshape(S5_G, t_len * S5_H, t_len * S5_H))
    kin = jnp.stack(kin)

    tt = jnp.arange(t_len)
    win, wout = [], []
    for d in range(2):
        e_in = (t_len - 1 - tt) if d == 0 else tt
        e_out = (tt + 1) if d == 0 else (t_len - tt)
        pr, pi = pw_re[d][..., e_in], pw_im[d][..., e_in]
        wr = pr[..., None] * bb_re[d][:, :, None] - pi[..., None] * bb_im[d][:, :, None]
        wi = pr[..., None] * bb_im[d][:, :, None] + pi[..., None] * bb_re[d][:, :, None]
        w = jnp.concatenate([wr, wi], axis=1)
        win.append(w.transpose(0, 2, 3, 1).reshape(S5_G, t_len * S5_H, 2 * S5_P))
        qr, qi = pw_re[d][..., e_out], pw_im[d][..., e_out]
        orr = c_re[d].transpose(0, 2, 1)[:, :, None] * qr[..., None] - c_im[d].transpose(0, 2, 1)[:, :, None] * qi[..., None]
        oii = -(c_re[d].transpose(0, 2, 1)[:, :, None] * qi[..., None] + c_im[d].transpose(0, 2, 1)[:, :, None] * qr[..., None])
        o = jnp.concatenate([orr, oii], axis=1)
        wout.append(o.reshape(S5_G, 2 * S5_P, t_len * S5_H))
    win, wout = jnp.stack(win), jnp.stack(wout)

    sc_re, sc_im = cpow(t_len * (2.0 ** jnp.arange(n_scan_steps, dtype=F32)))
    a1 = jnp.concatenate([sc_re, sc_re], axis=2)
    a2 = jnp.concatenate([-sc_im, sc_im], axis=2)
    pw = jnp.stack([a1, a2], axis=-1).transpose(0, 1, 3, 4, 2).reshape(2, S5_G, 2 * n_scan_steps, 2 * S5_P)
    cr, ci_ = (t[1] for t in cpow(t_len * (nc_lat - 1 - jnp.arange(nc_lat, dtype=F32))))
    tab = jnp.stack([jnp.concatenate([cr, cr], axis=1), jnp.concatenate([-ci_, ci_], axis=1)], axis=1)
    tab = jnp.pad(tab.transpose(0, 1, 3, 2), ((0, 0), (0, 0), (nc_ctx, 0), (0, 0)))
    return kin.astype(BF16), win.astype(BF16), wout.astype(BF16), pw, tab


S5_CB = 16
LANES = 128
S5_GPV = LANES // S5_H
S5_NV = BRANCH_W // LANES


def _lane_block_masks():
    blk = lax.broadcasted_iota(jnp.int32, (1, LANES), 1) // S5_H
    return [blk == m for m in range(S5_GPV)]


def _roll_lanes(x, shift):
    shift %= LANES
    return pltpu.roll(x, shift, 1) if shift else x


def _s5_pack_kernel(u_ref, x_ref):
    masks = _lane_block_masks()
    step = S5_T * S5_NV
    q = [[_roll_lanes(u_ref[0, pl.ds(t * S5_NV + j, S5_CB, stride=step), :], S5_H * t) for j in range(S5_NV)]
         for t in range(S5_T)]
    for j in range(S5_NV):
        for r in range(S5_GPV):
            for half in range(S5_T // S5_GPV):
                z = jnp.zeros((S5_CB, LANES), F32)
                for tp in range(S5_GPV):
                    z = jnp.where(masks[(r + tp) % S5_GPV], q[half * S5_GPV + tp][j], z)
                x_ref[j * S5_GPV + r, 0, :, half * LANES:(half + 1) * LANES] = _roll_lanes(z, -S5_H * r).astype(BF16)


def _s5_unpack_kernel(y_ref, o_ref):
    masks = _lane_block_masks()
    step = S5_T * S5_NV
    for j in range(S5_NV):
        for half in range(S5_T // S5_GPV):
            rl = [_roll_lanes(y_ref[j * S5_GPV + r, 0, :, half * LANES:(half + 1) * LANES], S5_H * r)
                  for r in range(S5_GPV)]
            for tp in range(S5_GPV):
                z = jnp.zeros((S5_CB, LANES), F32)
                for r in range(S5_GPV):
                    z = jnp.where(masks[(r + tp) % S5_GPV], rl[r], z)
                t = half * S5_GPV + tp
                o_ref[0, pl.ds(t * S5_NV + j, S5_CB, stride=step), :] = _roll_lanes(z, -S5_H * tp)


def _s5_kernel(x_ref, kin_ref, win_ref, wout_ref, pw_ref, tab_ref, y_ref, *, bsz, nc, nc_ctx, n_steps):
    rows = bsz * nc
    th = S5_T * S5_H
    sw = 2 * S5_P
    x = x_ref[0].reshape(rows, th)
    cidx = lax.broadcasted_iota(jnp.int32, (bsz, nc, sw), 1).reshape(rows, sw)
    seg_hi = jnp.where(cidx < nc_ctx, nc_ctx, nc)

    def cmul(a1, a2, h):
        return a1 * h + a2 * pltpu.roll(h, S5_P, 1)

    h = _dot(x, win_ref[0, 0])
    for i in range(n_steps):
        d = 1 << i
        a1 = pw_ref[0, 0, 2 * i:2 * i + 1, :]
        a2 = pw_ref[0, 0, 2 * i + 1:2 * i + 2, :]
        h = h + jnp.where(cidx >= d, cmul(a1, a2, pltpu.roll(h, d, 0)), 0.0)
    h_start = jnp.where(cidx >= 1, pltpu.roll(h, 1, 0), 0.0)
    y = _dot(x, kin_ref[0, 0]) + _dot(x, kin_ref[1, 0]) + _dot(h_start.astype(BF16), wout_ref[0, 0])

    h = _dot(x, win_ref[1, 0])
    for i in range(n_steps):
        d = 1 << i
        a1 = pw_ref[1, 0, 2 * i:2 * i + 1, :]
        a2 = pw_ref[1, 0, 2 * i + 1:2 * i + 2, :]
        h = h + jnp.where(cidx + d < seg_hi, cmul(a1, a2, pltpu.roll(h, rows - d, 0)), 0.0)
    h_ctx = jnp.broadcast_to(h.reshape(bsz, nc, sw)[:, 0:1, :], (bsz, nc, sw)).reshape(rows, sw)
    tab1 = jnp.broadcast_to(tab_ref[0, 0][None], (bsz, nc, sw)).reshape(rows, sw)
    tab2 = jnp.broadcast_to(tab_ref[0, 1][None], (bsz, nc, sw)).reshape(rows, sw)
    h_start = jnp.where(cidx + 1 < seg_hi, pltpu.roll(h, rows - 1, 0), 0.0) + cmul(tab1, tab2, h_ctx)
    y = y + _dot(h_start.astype(BF16), wout_ref[1, 0])
    y_ref[0] = y.reshape(bsz, nc, th)


def _s5_mix(u, weights, n_ctx):
    kin, win, wout, pw, tab = weights
    bsz, n, w = u.shape
    nc, nc_ctx = n // S5_T, n_ctx // S5_T
    th = S5_T * S5_H
    n_steps = pw.shape[2] // 2
    assert (1 << n_steps) >= nc and nc % S5_CB == 0 and nc_ctx >= 1
    tok_rows = S5_CB * S5_T * S5_NV
    x = pl.pallas_call(
        _s5_pack_kernel,
        grid=(bsz, nc // S5_CB),
        in_specs=[pl.BlockSpec((1, tok_rows, LANES), lambda b, c: (b, c, 0))],
        out_specs=pl.BlockSpec((S5_G, 1, S5_CB, th), lambda b, c: (0, b, c, 0)),
        out_shape=jax.ShapeDtypeStruct((S5_G, bsz, nc, th), BF16),
        compiler_params=_cparams(("parallel", "parallel"), 32),
        name="s5_pack",
    )(u.reshape(bsz, n * S5_NV, LANES))
    y = pl.pallas_call(
        functools.partial(_s5_kernel, bsz=bsz, nc=nc, nc_ctx=nc_ctx, n_steps=n_steps),
        grid=(S5_G,),
        in_specs=[pl.BlockSpec((1, bsz, nc, th), lambda g: (g, 0, 0, 0)),
                  pl.BlockSpec((2, 1, th, th), lambda g: (0, g, 0, 0)),
                  pl.BlockSpec((2, 1, th, 2 * S5_P), lambda g: (0, g, 0, 0)),
                  pl.BlockSpec((2, 1, 2 * S5_P, th), lambda g: (0, g, 0, 0)),
                  pl.BlockSpec((2, 1, 2 * n_steps, 2 * S5_P), lambda g: (0, g, 0, 0)),
                  pl.BlockSpec((1, 2, nc, 2 * S5_P), lambda g: (g, 0, 0, 0))],
        out_specs=pl.BlockSpec((1, bsz, nc, th), lambda g: (g, 0, 0, 0)),
        out_shape=jax.ShapeDtypeStruct((S5_G, bsz, nc, th), F32),
        compiler_params=_cparams(("parallel",), 40),
        name="s5_chunks",
    )(x, kin, win, wout, pw, tab)
    out = pl.pallas_call(
        _s5_unpack_kernel,
        grid=(bsz, nc // S5_CB),
        in_specs=[pl.BlockSpec((S5_G, 1, S5_CB, th), lambda b, c: (0, b, c, 0))],
        out_specs=pl.BlockSpec((1, tok_rows, LANES), lambda b, c: (b, c, 0)),
        out_shape=jax.ShapeDtypeStruct((bsz, n * S5_NV, LANES), F32),
        compiler_params=_cparams(("parallel", "parallel"), 32),
        name="s5_unpack",
    )(y)
    return out.reshape(bsz, n, w)


def _rope_tables(n_ctx, n_lat):
    quarter = HEAD_DIM // 4
    inv = 1.0 / (ROPE_BASE ** (jnp.arange(quarter, dtype=F32) / quarter))
    t = jnp.arange(n_lat, dtype=jnp.int32)
    ar = (t // GRID_W).astype(F32)[:, None] * inv[None, :]
    ac = (t % GRID_W).astype(F32)[:, None] * inv[None, :]
    cos = jnp.concatenate([jnp.cos(ar), jnp.cos(ar), jnp.cos(ac), jnp.cos(ac)], axis=1)
    sin = jnp.concatenate([-jnp.sin(ar), jnp.sin(ar), -jnp.sin(ac), jnp.sin(ac)], axis=1)
    cos = jnp.concatenate([jnp.ones((n_ctx, HEAD_DIM), F32), cos], axis=0)
    sin = jnp.concatenate([jnp.zeros((n_ctx, HEAD_DIM), F32), sin], axis=0)
    return jnp.concatenate([cos, cos], axis=1), jnp.concatenate([sin, sin], axis=1)


def _qk_prep_kernel(x_ref, cos_ref, sin_ref, qg_ref, kg_ref, j_ref, q_out, k_out, v_out):
    x = x_ref[0]
    wq, wk = ATT_HEADS * HEAD_DIM, ATT_KV_HEADS * HEAD_DIM
    cos, sin = cos_ref[...], sin_ref[...]

    def prep(t, g, w, scale):
        ss = _segsum(t * t, j_ref[0:w, 0:w])
        tn = t * lax.rsqrt(ss * (1.0 / HEAD_DIM) + NORM_EPS) * g
        reps = w // 128
        c = jnp.concatenate([cos] * reps, axis=1) if reps > 1 else cos
        s = jnp.concatenate([sin] * reps, axis=1) if reps > 1 else sin
        lane = lax.broadcasted_iota(jnp.int32, tn.shape, 1)
        first = (lane % 32) < 16
        swapped = jnp.where(first, pltpu.roll(tn, w - 16, 1), pltpu.roll(tn, 16, 1))
        return ((tn * c + swapped * s) * scale).astype(BF16)

    q = prep(x[:, 0:wq], qg_ref[...], wq, HEAD_DIM ** -0.5)
    k = prep(x[:, wq:wq + wk], kg_ref[...], wk, 1.0)
    v = x[:, wq + wk:wq + 2 * wk].astype(BF16)
    for h in range(ATT_HEADS):
        q_out[0, h] = q[:, h * HEAD_DIM:(h + 1) * HEAD_DIM]
    for h in range(ATT_KV_HEADS):
        k_out[0, h] = k[:, h * HEAD_DIM:(h + 1) * HEAD_DIM]
        v_out[0, h] = v[:, h * HEAD_DIM:(h + 1) * HEAD_DIM]


def _qk_prep(qkv, cos, sin, q_g, k_g, seg_j):
    bsz, n, w = qkv.shape
    tm = ROW_TILE
    wq, wk = ATT_HEADS * HEAD_DIM, ATT_KV_HEADS * HEAD_DIM
    return pl.pallas_call(
        _qk_prep_kernel,
        grid=(bsz, n // tm),
        in_specs=[pl.BlockSpec((1, tm, w), lambda b, j: (b, j, 0)),
                  pl.BlockSpec((tm, 128), lambda b, j: (j, 0)),
                  pl.BlockSpec((tm, 128), lambda b, j: (j, 0)),
                  pl.BlockSpec((1, wq), lambda b, j: (0, 0)),
                  pl.BlockSpec((1, wk), lambda b, j: (0, 0)),
                  pl.BlockSpec((BRANCH_W, BRANCH_W), lambda b, j: (0, 0))],
        out_specs=[pl.BlockSpec((1, ATT_HEADS, tm, HEAD_DIM), lambda b, j: (b, 0, j, 0)),
                   pl.BlockSpec((1, ATT_KV_HEADS, tm, HEAD_DIM), lambda b, j: (b, 0, j, 0)),
                   pl.BlockSpec((1, ATT_KV_HEADS, tm, HEAD_DIM), lambda b, j: (b, 0, j, 0))],
        out_shape=[jax.ShapeDtypeStruct((bsz, ATT_HEADS, n, HEAD_DIM), BF16),
                   jax.ShapeDtypeStruct((bsz, ATT_KV_HEADS, n, HEAD_DIM), BF16),
                   jax.ShapeDtypeStruct((bsz, ATT_KV_HEADS, n, HEAD_DIM), BF16)],
        compiler_params=_cparams(("parallel", "parallel"), 32),
        name="qk_prep",
    )(qkv, cos, sin, jnp.tile(q_g, ATT_HEADS).reshape(1, wq), jnp.tile(k_g, ATT_KV_HEADS).reshape(1, wk), seg_j)


def _attn_kernel(sink_ref, q_ref, k_ref, v_ref, o_ref, *, n_ctx, n_lat):
    i = pl.program_id(1)
    n_ctx_blocks = n_ctx // BLOCK
    rows = ATT_REP * BLOCK
    span = 3 * BLOCK
    kv_heads = range(ATT_KV_HEADS)
    rowi = lax.broadcasted_iota(jnp.int32, (rows, 1), 0)
    rep = rowi // BLOCK
    q = [q_ref[0, h * ATT_REP:(h + 1) * ATT_REP].reshape(rows, HEAD_DIM) for h in kv_heads]
    kc = [k_ref[0, h, 0:n_ctx, :] for h in kv_heads]
    vc = [v_ref[0, h, 0:n_ctx, :] for h in kv_heads]
    s_ctx = [_dot_nt(q[h], kc[h]) for h in kv_heads]
    sink = []
    for h in kv_heads:
        s = jnp.zeros((rows, 1), F32)
        for r in range(ATT_REP):
            s = jnp.where(rep == r, sink_ref[h * ATT_REP + r], s)
        sink.append(s)

    def store(o):
        o_ref[0] = jnp.concatenate([o[h][r * BLOCK:(r + 1) * BLOCK] for h in kv_heads for r in range(ATT_REP)], axis=1)

    @pl.when(i < n_ctx_blocks)
    def _():
        m = [jnp.maximum(jnp.max(s_ctx[h], axis=1, keepdims=True), sink[h]) for h in kv_heads]
        p = [jnp.exp(s_ctx[h] - m[h]) for h in kv_heads]
        den = [jnp.sum(p[h], axis=1, keepdims=True) + jnp.exp(sink[h] - m[h]) for h in kv_heads]
        store([_dot(p[h].astype(BF16), vc[h]) / den[h] for h in kv_heads])

    @pl.when(i >= n_ctx_blocks)
    def _():
        il = i - n_ctx_blocks
        start = pl.multiple_of(jnp.clip((il - 1) * BLOCK, 0, n_lat - span), BLOCK)
        qpos = il * BLOCK + rowi % BLOCK
        kpos = start + lax.broadcasted_iota(jnp.int32, (rows, span), 1)
        band = jnp.abs(qpos - kpos) <= WINDOW
        kw = [k_ref[0, h, pl.ds(n_ctx + start, span), :] for h in kv_heads]
        vw = [v_ref[0, h, pl.ds(n_ctx + start, span), :] for h in kv_heads]
        s_loc = [jnp.where(band, _dot_nt(q[h], kw[h]), NEG_INF) for h in kv_heads]
        m = [jnp.maximum(jnp.maximum(jnp.max(s_loc[h], axis=1, keepdims=True), jnp.max(s_ctx[h], axis=1, keepdims=True)),
                         sink[h]) for h in kv_heads]
        p_loc = [jnp.exp(s_loc[h] - m[h]) for h in kv_heads]
        p_ctx = [jnp.exp(s_ctx[h] - m[h]) for h in kv_heads]
        den = [jnp.sum(p_loc[h], axis=1, keepdims=True) + jnp.sum(p_ctx[h], axis=1, keepdims=True)
               + jnp.exp(sink[h] - m[h]) for h in kv_heads]
        store([(_dot(p_loc[h].astype(BF16), vw[h]) + _dot(p_ctx[h].astype(BF16), vc[h])) / den[h] for h in kv_heads])


def _attn(q, k, v, sink, n_ctx):
    bsz, _, n, _ = q.shape
    n_lat = n - n_ctx
    assert n_lat >= 3 * BLOCK and n_ctx % BLOCK == 0
    return pl.pallas_call(
        functools.partial(_attn_kernel, n_ctx=n_ctx, n_lat=n_lat),
        grid=(bsz, n // BLOCK),
        in_specs=[pl.BlockSpec(memory_space=pltpu.SMEM),
                  pl.BlockSpec((1, ATT_HEADS, BLOCK, HEAD_DIM), lambda b, i: (b, 0, i, 0)),
                  pl.BlockSpec((1, ATT_KV_HEADS, n, HEAD_DIM), lambda b, i: (b, 0, 0, 0)),
                  pl.BlockSpec((1, ATT_KV_HEADS, n, HEAD_DIM), lambda b, i: (b, 0, 0, 0))],
        out_specs=pl.BlockSpec((1, BLOCK, ATT_HEADS * HEAD_DIM), lambda b, i: (b, i, 0)),
        out_shape=jax.ShapeDtypeStruct((bsz, n, ATT_HEADS * HEAD_DIM), F32),
        compiler_params=_cparams(("parallel", "parallel"), 32),
        name="window_attn",
    )(sink, q, k, v)


RW_FEATS = 6
RW_PASSES_SCORE = 1
RW_PASSES_INV = 1
RW_PASSES_MIX = 1
RW_PASSES_STATE = 3
RW_PASSES_OUT = 1


def _softplus(y):
    return jnp.maximum(y, 0.0) + jnp.log(1.0 + jnp.exp(-jnp.abs(y)))


def _rw_feat_kernel(x_ref, p_ref, n_ref, mup_ref, mun_ref, w2_ref, a2_ref, w0_ref, a0_ref, kk_ref, ka_ref, rk_ref,
                    j_ref, f_out, v_out, gc_out, bv_out, *, tm, n_ctx, n_tot):
    w = BRANCH_W
    z = x_ref[0]
    rowi = lax.broadcasted_iota(jnp.int32, (tm, 1), 0)
    pos = pl.program_id(1) * tm + rowi
    prev, nxt = _shift_rows(z, p_ref[0, 7:8, :], n_ref[0, 0:1, :],
                            (pos == 0) | (pos == n_ctx), (pos == n_ctx - 1) | (pos == n_tot - 1))
    zz = z + mup_ref[...] * (prev - z) + mun_ref[...] * (nxt - z)
    r, k, v, lo = zz[:, 0:w], zz[:, w:2 * w], zz[:, 2 * w:3 * w], zz[:, 3 * w:]
    seg_j = j_ref[...]
    kk = k * kk_ref[...]
    kk = kk * lax.rsqrt(_segsum(kk * kk, seg_j) + 1e-12)
    w_log = -_softplus(-(_mx(jnp.tanh(lo), w2_ref[...], NN, 3) + w0_ref[...])) - 0.5
    log_decay = -jnp.exp(w_log)
    a = jax.nn.sigmoid(_mx(lo, a2_ref[...], NN, 3) + a0_ref[...])
    ri = lax.broadcasted_iota(jnp.int32, (tm, tm), 0)
    ci = lax.broadcasted_iota(jnp.int32, (tm, tm), 1)
    same = (ri // RW_C) == (ci // RW_C)
    tot_m = jnp.where(same, 1.0, 0.0).astype(BF16)
    bonus = jnp.zeros((tm, w), F32)
    for d in range(2):
        ld = log_decay[:, d * w:(d + 1) * w]
        ad = a[:, d * w:(d + 1) * w]
        kd = k * (1.0 + (ad - 1.0) * ka_ref[...])
        b = kk * ad
        bonus = bonus + r * kd * rk_ref[...]
        tri = jnp.where(same & ((ci <= ri) if d == 0 else (ci >= ri)), 1.0, 0.0).astype(BF16)
        cum = _mx_exact_lhs(tri, ld)
        tot = _mx_exact_lhs(tot_m, ld)
        e_neg = jnp.exp(-cum)
        e_hat = jnp.exp(tot - cum)
        feats = (kk * jnp.exp(cum - ld), b * e_neg, kd * e_neg, r * jnp.exp(cum), kd * e_hat, b * e_hat)
        for q in range(RW_FEATS):
            for h in range(RW_HEADS):
                f_out[0, d, q, h] = feats[q][:, h * RW_N:(h + 1) * RW_N].astype(BF16)
        g = jnp.exp(tot)
        gc_out[0, d, 0] = jnp.concatenate([g[c * RW_C:c * RW_C + 1] for c in range(tm // RW_C)], axis=0)
    for h in range(RW_HEADS):
        v_out[0, h] = v[:, h * RW_N:(h + 1) * RW_N].astype(BF16)
    bv_out[0] = _segsum(bonus, seg_j) * v


def _rw_features(rw, params, seg_j, n_ctx):
    mu_prev, mu_next, w0, w2, a0, a2, k_k, k_a, r_k = params
    bsz, n, wz = rw.shape
    tm = ROW_TILE
    w = BRANCH_W
    nb8 = n // 8
    lw = 4 * RW_LORA
    w2p = jnp.zeros((lw, 2 * w), F32)
    a2p = jnp.zeros((lw, 2 * w), F32)
    for d in range(2):
        w2p = w2p.at[d * RW_LORA:(d + 1) * RW_LORA, d * w:(d + 1) * w].set(w2[d])
        a2p = a2p.at[(2 + d) * RW_LORA:(3 + d) * RW_LORA, d * w:(d + 1) * w].set(a2[d])
    vec = lambda t: t.reshape(1, -1)
    full = lambda shape: pl.BlockSpec(shape, lambda b, j: (0,) * len(shape))
    gpt = tm // RW_C
    return pl.pallas_call(
        functools.partial(_rw_feat_kernel, tm=tm, n_ctx=n_ctx, n_tot=n),
        grid=(bsz, n // tm),
        in_specs=[pl.BlockSpec((1, tm, wz), lambda b, j: (b, j, 0)),
                  pl.BlockSpec((1, 8, wz), lambda b, j: (b, jnp.maximum(j * (tm // 8) - 1, 0), 0)),
                  pl.BlockSpec((1, 8, wz), lambda b, j: (b, jnp.minimum((j + 1) * (tm // 8), nb8 - 1), 0)),
                  full((1, wz)), full((1, wz)), full((lw, 2 * w)), full((lw, 2 * w)),
                  full((1, 2 * w)), full((1, 2 * w)), full((1, w)), full((1, w)), full((1, w)), full((w, w))],
        out_specs=[pl.BlockSpec((1, 2, RW_FEATS, RW_HEADS, tm, RW_N), lambda b, j: (b, 0, 0, 0, j, 0)),
                   pl.BlockSpec((1, RW_HEADS, tm, RW_N), lambda b, j: (b, 0, j, 0)),
                   pl.BlockSpec((1, 2, 1, gpt, w), lambda b, j: (b, 0, j, 0, 0)),
                   pl.BlockSpec((1, tm, w), lambda b, j: (b, j, 0))],
        out_shape=[jax.ShapeDtypeStruct((bsz, 2, RW_FEATS, RW_HEADS, n, RW_N), BF16),
                   jax.ShapeDtypeStruct((bsz, RW_HEADS, n, RW_N), BF16),
                   jax.ShapeDtypeStruct((bsz, 2, n // tm, gpt, w), F32),
                   jax.ShapeDtypeStruct((bsz, n, w), F32)],
        compiler_params=_cparams(("parallel", "parallel"), 56),
        name="rwkv_features",
    )(rw, rw, rw, vec(mu_prev), vec(mu_next), w2p, a2p, vec(w0), vec(a0), vec(k_k), vec(k_a), vec(r_k), seg_j)


def _tri_inverse(a_list, ri, ci):
    mm = lambda ps, qs: [_mx(p, q, NN, RW_PASSES_INV) for p, q in zip(ps, qs)]
    eye = jnp.where(ri == ci, 1.0, 0.0)
    x = [jnp.where((ri // 8) == (ci // 8), -a, 0.0) for a in a_list]
    p = [eye + t for t in x]
    x2 = mm(x, x)
    p = [s + t for s, t in zip(p, mm(p, x2))]
    x4 = mm(x2, x2)
    p = [s + t for s, t in zip(p, mm(p, x4))]
    blk = 8
    while blk < RW_C:
        off = ((ri // (2 * blk)) == (ci // (2 * blk))) & ((ri // blk) != (ci // blk))
        e = [jnp.where(off, a, 0.0) for a in a_list]
        p = [s - t for s, t in zip(p, mm(mm(p, e), p))]
        blk *= 2
    return p


def _rw_chunk_kernel(ff_ref, fb_ref, vf_ref, vb_ref, gf_ref, gb_ref, yf_ref, yb_ref, state, *, nch, nch_ctx):
    s = pl.program_id(1)
    rvs = jnp.where(s < nch_ctx, nch_ctx - 1 - s, nch + nch_ctx - 1 - s)
    gpt = gf_ref.shape[3]
    nk = RW_N

    @pl.when(s == 0)
    def _():
        state[...] = jnp.zeros_like(state)

    ri = lax.broadcasted_iota(jnp.int32, (RW_C, RW_C), 0)
    ci = lax.broadcasted_iota(jnp.int32, (RW_C, RW_C), 1)
    eye = ri == ci
    before = ((ci < ri), (ci > ri))
    upto = tuple(m | eye for m in before)
    f_refs, v_refs = (ff_ref, fb_ref), (vf_ref, vb_ref)
    g_all = [gf_ref[0, 0, 0, pl.ds(s % gpt, 1), :], gb_ref[0, 0, 0, pl.ds(rvs % gpt, 1), :]]
    ent = [(d, h) for d in range(2) for h in range(RW_HEADS)]
    idx = range(len(ent))
    kap, bet, kt, rt, kh, bh = ([f_refs[d][0, 0, q, h] for d, h in ent] for q in range(RW_FEATS))
    v = [v_refs[d][0, h] for d, h in ent]
    kr = [jnp.concatenate([kap[i], rt[i]], axis=0) for i in idx]
    sb = [_mx(kr[i], bet[i], NT, RW_PASSES_SCORE) for i in idx]
    sk = [_mx(kr[i], kt[i], NT, RW_PASSES_SCORE) for i in idx]
    a_ab = [jnp.where(before[ent[i][0]], sb[i][0:RW_C], 0.0) for i in idx]
    a_qb = [jnp.where(upto[ent[i][0]], sb[i][RW_C:], 0.0) for i in idx]
    a_k = [jnp.concatenate([jnp.where(before[ent[i][0]], sk[i][0:RW_C], 0.0),
                            jnp.where(upto[ent[i][0]], sk[i][RW_C:], 0.0)], axis=0) for i in idx]
    akv = [_mx(a_k[i], v[i], NN, RW_PASSES_MIX) for i in idx]
    vk = [_mx(v[i], kh[i], TN, RW_PASSES_MIX) for i in idx]
    t_inv = _tri_inverse(a_ab, ri, ci)
    wu = [_mx(t_inv[i], jnp.concatenate([kap[i].astype(F32), akv[i][0:RW_C]], axis=1), NN, RW_PASSES_MIX)
          for i in idx]
    wub = [_mx(wu[i], bh[i], TN, RW_PASSES_MIX) for i in idx]
    aq = [_mx(a_qb[i], wu[i], NN, RW_PASSES_MIX) for i in idx]
    s0 = [state[d, h] for d, h in ent]
    out = [_mx(rt[i] - aq[i][:, 0:nk], s0[i], NT, RW_PASSES_OUT) + (akv[i][RW_C:] - aq[i][:, nk:]) for i in idx]
    m_c = [jnp.where(eye, g_all[d][:, h * nk:(h + 1) * nk], 0.0) - wub[i][0:nk] for i, (d, h) in enumerate(ent)]
    s1 = [_mx(s0[i], m_c[i], NN, RW_PASSES_STATE) + (vk[i] - wub[i][nk:]) for i in idx]
    for i, (d, h) in enumerate(ent):
        state[d, h] = s1[i]
    yf_ref[0] = jnp.concatenate(out[0:RW_HEADS], axis=1)
    yb_ref[0] = jnp.concatenate(out[RW_HEADS:], axis=1)


def _rw_chunks(feats, v, gc, n_ctx):
    bsz, _, _, _, n, _ = feats.shape
    nch, nch_ctx = n // RW_C, n_ctx // RW_C
    gpt = gc.shape[3]

    def rv(s):
        return jnp.where(s < nch_ctx, nch_ctx - 1 - s, nch + nch_ctx - 1 - s)

    ident = lambda s: s
    f_spec = lambda d, f: pl.BlockSpec((1, 1, RW_FEATS, RW_HEADS, RW_C, RW_N), lambda b, s: (b, d, 0, 0, f(s), 0))
    v_spec = lambda f: pl.BlockSpec((1, RW_HEADS, RW_C, RW_N), lambda b, s: (b, 0, f(s), 0))
    g_spec = lambda d, f: pl.BlockSpec((1, 1, 1, gpt, BRANCH_W), lambda b, s: (b, d, f(s) // gpt, 0, 0))
    return pl.pallas_call(
        functools.partial(_rw_chunk_kernel, nch=nch, nch_ctx=nch_ctx),
        grid=(bsz, nch),
        in_specs=[f_spec(0, ident), f_spec(1, rv), v_spec(ident), v_spec(rv), g_spec(0, ident), g_spec(1, rv)],
        out_specs=[pl.BlockSpec((1, RW_C, BRANCH_W), lambda b, s: (b, s, 0)),
                   pl.BlockSpec((1, RW_C, BRANCH_W), lambda b, s: (b, rv(s), 0))],
        out_shape=[jax.ShapeDtypeStruct((bsz, n, BRANCH_W), F32), jax.ShapeDtypeStruct((bsz, n, BRANCH_W), F32)],
        scratch_shapes=[pltpu.VMEM((2, RW_HEADS, RW_N, RW_N), F32)],
        compiler_params=_cparams(("parallel", "arbitrary"), 32),
        name="rwkv_chunks",
    )(feats, feats, v, v, gc, gc)


def _conv3_kernel(x_ref, p_ref, n_ref, w_ref, b_ref, o_ref, *, tm, n_blocks):
    j = pl.program_id(1)
    z = x_ref[0]
    rowi = lax.broadcasted_iota(jnp.int32, (tm, 1), 0)
    prev, nxt = _shift_rows(z, p_ref[0, 7:8, :], n_ref[0, 0:1, :],
                            (rowi == 0) & (j == 0), (rowi == tm - 1) & (j == n_blocks - 1))
    o_ref[0] = prev * w_ref[0:1, :] + z * w_ref[1:2, :] + nxt * w_ref[2:3, :] + b_ref[...]


def _hy_conv3(hy, conv_w, conv_b, seg_start, seg_len):
    bsz, _, w = hy.shape
    tm = ROW_TILE
    off, nb = seg_start // tm, seg_len // tm
    off8, nb8 = seg_start // 8, seg_len // 8
    return pl.pallas_call(
        functools.partial(_conv3_kernel, tm=tm, n_blocks=nb),
        grid=(bsz, nb),
        in_specs=[pl.BlockSpec((1, tm, w), lambda b, j: (b, off + j, 0)),
                  pl.BlockSpec((1, 8, w), lambda b, j: (b, off8 + jnp.maximum(j * (tm // 8) - 1, 0), 0)),
                  pl.BlockSpec((1, 8, w), lambda b, j: (b, off8 + jnp.minimum((j + 1) * (tm // 8), nb8 - 1), 0)),
                  pl.BlockSpec((3, w), lambda b, j: (0, 0)),
                  pl.BlockSpec((1, w), lambda b, j: (0, 0))],
        out_specs=pl.BlockSpec((1, tm, w), lambda b, j: (b, j, 0)),
        out_shape=jax.ShapeDtypeStruct((bsz, seg_len, w), F32),
        compiler_params=_cparams(("parallel", "parallel"), 32),
        name="hyena_conv3",
    )(hy, hy, hy, conv_w, conv_b.reshape(1, w))


def _hy_embedding(n):
    t = jnp.linspace(0.0, 1.0, n, dtype=F32)[:, None]
    ang = 2.0 * math.pi * jnp.arange(n, dtype=F32)[:, None] / n
    bands = jnp.linspace(1e-4, HY_BANDS - 1, HY_BANDS, dtype=F32)[None, :]
    z = jnp.concatenate([t, jnp.cos(bands * ang), -jnp.sin(bands * ang)], axis=-1)
    return jnp.pad(z, ((0, 0), (0, HY_FFN - HY_EMB)))


def _hy_filter_kernel(e_ref, w1_ref, b1_ref, f1_ref, w2_ref, b2_ref, f2_ref, w3_ref, dl_ref, o_ref):
    e = e_ref[...]
    h = jnp.sin(f1_ref[...] * (_dot(e, w1_ref[...], HI) + b1_ref[...]))
    h = jnp.sin(f2_ref[...] * (_dot(h, w2_ref[...], HI) + b2_ref[...]))
    o_ref[...] = _dot(h, w3_ref[...], HI) * jnp.exp(-e[:, 0:1] * dl_ref[...])


def _hy_filters(n, w1, b1, f1, w2, b2, f2, w3):
    tm = min(ROW_TILE, n)
    wf = w3.shape[1]
    emb = _hy_embedding(n)
    w1p = jnp.pad(w1, ((0, HY_FFN - HY_EMB), (0, 0)))
    deltas = jnp.abs(jnp.linspace(math.log(HY_TARGET) / HY_SLOW_PCT, math.log(HY_TARGET) / HY_FAST_PCT,
                                  BRANCH_W, dtype=F32))
    dl = jnp.tile(deltas, wf // BRANCH_W).reshape(1, wf)
    vec = lambda t: t.reshape(1, -1)
    full = lambda shape: pl.BlockSpec(shape, lambda i: (0,) * len(shape))
    return pl.pallas_call(
        _hy_filter_kernel,
        grid=(n // tm,),
        in_specs=[pl.BlockSpec((tm, HY_FFN), lambda i: (i, 0)),
                  full((HY_FFN, HY_FFN)), full((1, HY_FFN)), full((1, HY_FFN)),
                  full((HY_FFN, HY_FFN)), full((1, HY_FFN)), full((1, HY_FFN)),
                  full((HY_FFN, wf)), full((1, wf))],
        out_specs=pl.BlockSpec((tm, wf), lambda i: (i, 0)),
        out_shape=jax.ShapeDtypeStruct((n, wf), F32),
        compiler_params=_cparams(("parallel",), 32),
        name="hyena_filters",
    )(emb, w1p, vec(b1), vec(f1), w2, vec(b2), vec(f2), w3, dl)


def _dft_mats(n):
    k = jnp.arange(n, dtype=jnp.int32)
    ang = (math.pi / n) * ((k[:, None] * k[None, :]) % (2 * n)).astype(F32)
    alt = jnp.where(k % 2 == 0, 1.0, -1.0).astype(F32)
    cos, sin = jnp.cos(ang), jnp.sin(ang)
    fwd = jnp.concatenate([cos, (-sin).at[0].set(alt)], axis=0)
    wk = jnp.where(k == 0, 1.0, 2.0).astype(F32)[None, :]
    inv = jnp.concatenate([wk * cos, (-2.0 * sin).at[:, 0].set(alt)], axis=1) / (2 * n)
    return fwd.astype(BF16), inv.astype(BF16)


def _hy_spec_kernel(fc_ref, fs_ref, h_ref, g_ref, *, tk):
    h = h_ref[...]
    n, w2 = h.shape
    w = w2 // 2
    rowi = lax.broadcasted_iota(jnp.int32, (n, w2), 0)
    coli = lax.broadcasted_iota(jnp.int32, (n, w2), 1)
    h = jnp.where((rowi == 0) & (coli >= w), 0.0, h)
    hi = h.astype(BF16)
    lo = (h - hi.astype(F32)).astype(BF16)
    sr = _dot(fc_ref[...], hi) + _dot(fc_ref[...], lo)
    si = _dot(fs_ref[...], hi) + _dot(fs_ref[...], lo)
    krow = pl.program_id(1) * tk + lax.broadcasted_iota(jnp.int32, (tk, 1), 0)
    g_ref[0, 0] = sr[:, 0:w] + sr[:, w:w2]
    g_ref[0, 1] = si[:, 0:w] + jnp.where(krow == 0, 1.0, -1.0) * si[:, w:w2]


def _hy_spectrum(filt, fwd, n):
    w = BRANCH_W
    tk = min(ROW_TILE, n)
    nk = n // tk
    return pl.pallas_call(
        functools.partial(_hy_spec_kernel, tk=tk),
        grid=(2, nk),
        in_specs=[pl.BlockSpec((tk, n), lambda j, kt: (kt, 0)),
                  pl.BlockSpec((tk, n), lambda j, kt: (nk + kt, 0)),
                  pl.BlockSpec((n, 2 * w), lambda j, kt: (0, j))],
        out_specs=pl.BlockSpec((1, 2, tk, w), lambda j, kt: (j, 0, kt, 0)),
        out_shape=jax.ShapeDtypeStruct((2, 2, n, w), F32),
        compiler_params=_cparams(("parallel", "parallel"), 48),
        name="hyena_filter_spectrum",
    )(fwd, fwd, filt)


HY_BATCH = 2


def _hy_fwd_kernel(fc_ref, fs_ref, u_ref, g_ref, z_ref):
    w = BRANCH_W
    u = jnp.concatenate([u_ref[i] for i in range(HY_BATCH)], axis=1).astype(BF16)
    ur = _dot(fc_ref[...], u)
    ui = _dot(fs_ref[...], u)
    gr = jnp.concatenate([g_ref[0]] * HY_BATCH, axis=1)
    gi = jnp.concatenate([g_ref[1]] * HY_BATCH, axis=1)
    tk = ur.shape[0]
    first = (pl.program_id(1) == 0) & (lax.broadcasted_iota(jnp.int32, (tk, 1), 0) == 0)
    zr = jnp.where(first, ur * gr, ur * gr - ui * gi).astype(BF16)
    zi = jnp.where(first, ui * gi, ur * gi + ui * gr).astype(BF16)
    for i in range(HY_BATCH):
        z_ref[i, 0] = zr[:, i * w:(i + 1) * w]
        z_ref[i, 1] = zi[:, i * w:(i + 1) * w]


def _hy_inv_kernel(gi_ref, z_ref, u_ref, x_ref, skip_ref, o_ref):
    w = BRANCH_W
    y = _dot(gi_ref[...], jnp.concatenate([z_ref[i] for i in range(HY_BATCH)], axis=1))
    for i in range(HY_BATCH):
        o_ref[i] = x_ref[i] * (y[:, i * w:(i + 1) * w] + skip_ref[...] * u_ref[i])


def _hy_long_conv(src, src_col, gate_src, gate_col, spec_j, skip_j, fwd, inv):
    bsz, n, _ = src.shape
    w = BRANCH_W
    hb = HY_BATCH
    assert bsz % hb == 0
    tk = min(ROW_TILE, n)
    nk = n // tk
    z = pl.pallas_call(
        _hy_fwd_kernel,
        grid=(bsz // hb, nk),
        in_specs=[pl.BlockSpec((tk, n), lambda b, kt: (kt, 0)),
                  pl.BlockSpec((tk, n), lambda b, kt: (nk + kt, 0)),
                  pl.BlockSpec((hb, n, w), lambda b, kt: (b, 0, src_col)),
                  pl.BlockSpec((2, tk, w), lambda b, kt: (0, kt, 0))],
        out_specs=pl.BlockSpec((hb, 2, tk, w), lambda b, kt: (b, 0, kt, 0)),
        out_shape=jax.ShapeDtypeStruct((bsz, 2, n, w), BF16),
        compiler_params=_cparams(("parallel", "parallel"), 56),
        name="hyena_dft",
    )(fwd, fwd, src, spec_j)
    z = z.reshape(bsz, 2 * n, w)
    return pl.pallas_call(
        _hy_inv_kernel,
        grid=(bsz // hb, nk),
        in_specs=[pl.BlockSpec((tk, 2 * n), lambda b, t: (t, 0)),
                  pl.BlockSpec((hb, 2 * n, w), lambda b, t: (b, 0, 0)),
                  pl.BlockSpec((hb, tk, w), lambda b, t: (b, t, src_col)),
                  pl.BlockSpec((hb, tk, w), lambda b, t: (b, t, gate_col)),
                  pl.BlockSpec((1, w), lambda b, t: (0, 0))],
        out_specs=pl.BlockSpec((hb, tk, w), lambda b, t: (b, t, 0)),
        out_shape=jax.ShapeDtypeStruct((bsz, n, w), F32),
        compiler_params=_cparams(("parallel", "parallel"), 56),
        name="hyena_idft",
    )(inv, z, src, gate_src, skip_j.reshape(1, w))


def _hyena_segment(hy, seg_start, seg_len, conv_w, conv_b, ffn, skip):
    z = _hy_conv3(hy, conv_w, conv_b, seg_start, seg_len)
    filt = _hy_filters(seg_len, *ffn)
    fwd, inv = _dft_mats(seg_len)
    spec = _hy_spectrum(filt, fwd, seg_len)
    y1 = _hy_long_conv(z, 0, z, 1, spec[0], skip[0], fwd, inv)
    return _hy_long_conv(y1, 0, z, 2, spec[1], skip[1], fwd, inv)


def _merge_kernel(x_ref, gx_ref, gc_ref, s5u_ref, s5y_ref, d_ref, gw_ref, gb_ref, att_ref,
                  rwf_ref, rwb_ref, bv_ref, lng_ref, lnb_ref, hyl_ref, hyc_ref, gate_ref, bg_ref, wo_ref, j_ref,
                  o_ref, *, n_ctx_blocks, blk_off):
    w = BRANCH_W
    is_ctx = (pl.program_id(1) + blk_off) < n_ctx_blocks

    def rms(y, g):
        return y * lax.rsqrt(jnp.mean(y * y, axis=-1, keepdims=True) + NORM_EPS) * g

    ys = d_ref[...] * s5u_ref[0] + s5y_ref[0]
    ys = 0.5 * ys * (1.0 + lax.erf(ys * (2.0 ** -0.5)))
    ys = ys * jax.nn.sigmoid(_dot(ys.astype(BF16), gw_ref[...]) + gb_ref[...])
    ys = rms(ys, bg_ref[0:1, :])
    ya = rms(att_ref[0], bg_ref[1:2, :])
    seg_j = j_ref[...]
    yr = rwf_ref[0] + rwb_ref[0]
    mu = _segsum(yr, seg_j) * (1.0 / RW_N)
    yc = yr - mu
    var = _segsum(yc * yc, seg_j) * (1.0 / RW_N)
    yr = yc * lax.rsqrt(var + RW_LN_EPS) * lng_ref[...] + lnb_ref[...] + bv_ref[0]
    yh = rms(jnp.where(is_ctx, hyc_ref[0], hyl_ref[0]), bg_ref[2:3, :])
    gp = gate_ref[0]
    sg = gp * jax.nn.sigmoid(gp)
    acc = _dot((ys * sg[:, 0:w]).astype(BF16), wo_ref[0:w, :])
    acc += _dot((ya * sg[:, w:2 * w]).astype(BF16), wo_ref[w:2 * w, :])
    acc += _dot((yr * sg[:, 2 * w:3 * w]).astype(BF16), wo_ref[2 * w:3 * w, :])
    acc += _dot((yh * sg[:, 3 * w:4 * w]).astype(BF16), wo_ref[3 * w:4 * w, :])
    gate = jnp.where(is_ctx, gc_ref[0, 0], gx_ref[0, 0])
    o_ref[0] = x_ref[0] + gate * acc


def _merge(x_all, mod_l, s5u, s5y, d_skip, glu_w, glu_b, att, rwf, rwb, bv, ln_g, ln_b, hy_lat, hy_ctx,
           gate_pre, branch_g, w_out, seg_j, n_ctx, with_ctx):
    bsz, n, d = x_all.shape
    w = BRANCH_W
    tm = ROW_TILE
    ncb = n_ctx // tm
    off = 0 if with_ctx else ncb
    nb = n // tm - off
    nlb = (n - n_ctx) // tm
    ctx_row = bsz
    row = lambda width: pl.BlockSpec((1, tm, width), lambda b, j: (b, j + off, 0))
    full = lambda shape: pl.BlockSpec(shape, lambda b, j: (0,) * len(shape))
    if hy_ctx is None:
        hy_ctx = hy_lat
        hyc_spec = pl.BlockSpec((1, tm, w), lambda b, j: (b, 0, 0))
    else:
        hyc_spec = pl.BlockSpec((1, tm, w), lambda b, j: (b, jnp.minimum(j + off, ncb - 1), 0))
    vec = lambda t: t.reshape(1, -1)
    return pl.pallas_call(
        functools.partial(_merge_kernel, n_ctx_blocks=ncb, blk_off=off),
        grid=(bsz, nb),
        in_specs=[row(d),
                  pl.BlockSpec((1, 1, 1, d), lambda b, j: (b, 2, 0, 0)),
                  pl.BlockSpec((1, 1, 1, d), lambda b, j: (ctx_row, 2, 0, 0)),
                  row(w), row(w), full((1, w)), full((w, w)), full((1, w)),
                  row(w), row(w), row(w), row(w), full((1, w)), full((1, w)),
                  pl.BlockSpec((1, tm, w), lambda b, j: (b, jnp.clip(j + off - ncb, 0, nlb - 1), 0)),
                  hyc_spec, row(4 * w), full((3, w)), full((4 * w, d)), full((w, w))],
        out_specs=pl.BlockSpec((1, tm, d), lambda b, j: (b, j, 0)),
        out_shape=jax.ShapeDtypeStruct((bsz, nb * tm, d), F32),
        compiler_params=_cparams(("parallel", "parallel"), 56),
        name="merge_out_proj",
    )(x_all, mod_l, mod_l, s5u, s5y, vec(d_skip), glu_w.astype(BF16), vec(glu_b), att,
      rwf, rwb, bv, vec(ln_g), vec(ln_b), hy_lat, hy_ctx, gate_pre, branch_g, w_out, seg_j)


def kernel(x, c, ctx, c_ctx, norm_g, w_ada, b_ada, w_in, w_out, branch_g, s5_lam_re, s5_lam_im, s5_log_step, s5_b_re, s5_b_im, s5_c_re, s5_c_im, s5_d, s5_glu_w, s5_glu_b, att_q_g, att_k_g, att_sink, rw_mu_prev, rw_mu_next, rw_w0, rw_w2, rw_a0, rw_a2, rw_k_k, rw_k_a, rw_r_k, rw_ln_g, rw_ln_b, hy_conv_w, hy_conv_b, hy_w1, hy_b1, hy_f1, hy_w2, hy_b2, hy_f2, hy_w3, hy_skip):
    bsz, n_lat, d = x.shape
    n_ctx = ctx.shape[1]
    depth = w_ada.shape[0]
    n = n_ctx + n_lat
    assert bsz + 1 <= MOD_ROWS and n_ctx % ROW_TILE == 0 and n_lat % ROW_TILE == 0

    c_all = jnp.zeros((MOD_ROWS, d), F32).at[0:bsz].set(c).at[bsz].set(c_ctx)
    mod = _ada_mod(c_all, w_ada, b_ada).reshape(depth, MOD_ROWS, 3, 1, d)
    w_in_b = w_in.astype(BF16)
    w_out_b = w_out.astype(BF16)
    rope_cos, rope_sin = _rope_tables(n_ctx, n_lat)
    lane = jnp.arange(BRANCH_W)
    seg_j = (lane[:, None] // HEAD_DIM == lane[None, :] // HEAD_DIM).astype(BF16)
    n_scan_steps = max(1, math.ceil(math.log2(n // S5_T)))
    slabs = (P_S5, P_QKV, P_RW, P_HY, P_GATE)
    tiles = (P_S5, P_QKV, P_RW, P_HY // 2, P_GATE // 2)
    offs = [sum(slabs[:i]) for i in range(len(slabs))]

    x_all = jnp.concatenate([ctx, x], axis=1)
    for l in range(depth):
        with_ctx = l < depth - 1
        h = _norm_mod(x_all, norm_g[l], mod[l], n_ctx).reshape(bsz * n, d)
        s5u, qkv, rw, hy, gate_pre = (
            _mm(h, w_in_b[l][:, o:o + wd], 1024, tn).reshape(bsz, n, wd) for o, wd, tn in zip(offs, slabs, tiles))

        s5_w = _s5_weights(s5_lam_re[l], s5_lam_im[l], s5_log_step[l], s5_b_re[l], s5_b_im[l],
                           s5_c_re[l], s5_c_im[l], n_scan_steps, n_lat // S5_T, n_ctx // S5_T)
        s5y = _s5_mix(s5u, s5_w, n_ctx)

        qh, kh, vh = _qk_prep(qkv, rope_cos, rope_sin, att_q_g[l], att_k_g[l], seg_j)
        att = _attn(qh, kh, vh, att_sink[l], n_ctx)

        feats, rv, gc, bv = _rw_features(
            rw, (rw_mu_prev[l], rw_mu_next[l], rw_w0[l], rw_w2[l], rw_a0[l], rw_a2[l], rw_k_k[l], rw_k_a[l],
                 rw_r_k[l].reshape(-1)), seg_j, n_ctx)
        rwf, rwb = _rw_chunks(feats, rv, gc, n_ctx)

        ffn = (hy_w1[l], hy_b1[l], hy_f1[l], hy_w2[l], hy_b2[l], hy_f2[l], hy_w3[l])
        hy_lat = _hyena_segment(hy, n_ctx, n_lat, hy_conv_w[l], hy_conv_b[l], ffn, hy_skip[l])
        hy_ctx = _hyena_segment(hy, 0, n_ctx, hy_conv_w[l], hy_conv_b[l], ffn, hy_skip[l]) if with_ctx else None

        x_all = _merge(x_all, mod[l], s5u, s5y, s5_d[l].reshape(-1), s5_glu_w[l], s5_glu_b[l], att,
                       rwf, rwb, bv, rw_ln_g[l], rw_ln_b[l], hy_lat, hy_ctx, gate_pre, branch_g[l], w_out_b[l],
                       seg_j, n_ctx, with_ctx)
    return x_all
```

```python
import functools
import math

import jax
import jax.numpy as jnp
from jax import lax
from jax.experimental import pallas as pl
from jax.experimental.pallas import tpu as pltpu

F32 = jnp.float32
BF16 = jnp.bfloat16
HI = lax.Precision.HIGHEST

D_MODEL = 2048
GRID_W = 64
BRANCH_W = 512
NORM_EPS = 1e-6
S5_H = 16
S5_G = BRANCH_W // S5_H
S5_P = 64
HEAD_DIM = 64
ATT_HEADS = 8
ATT_KV_HEADS = 2
ATT_REP = ATT_HEADS // ATT_KV_HEADS
WINDOW = 128
BLOCK = 128
ROPE_BASE = 10000.0
NEG_INF = -1e30
RW_N = 64
RW_HEADS = 8
RW_LORA = 32
RW_LN_EPS = 64e-5
HY_EMB = 33
HY_BANDS = 16
HY_FFN = 64
HY_TARGET = 1e-2
HY_FAST_PCT = 0.3
HY_SLOW_PCT = 1.5
P_S5 = BRANCH_W
P_QKV = ATT_HEADS * HEAD_DIM + 2 * ATT_KV_HEADS * HEAD_DIM
P_RW = 3 * BRANCH_W + 4 * RW_LORA
P_HY = 3 * BRANCH_W
P_GATE = 4 * BRANCH_W

V7X_VMEM_BYTES = 64 * 1024 * 1024
S5_T = 16
RW_C = 64
ROW_TILE = 256
MOD_ROWS = 24


def _cparams(sem, vmem_mb):
    assert vmem_mb * 1024 * 1024 <= V7X_VMEM_BYTES
    return pltpu.CompilerParams(dimension_semantics=sem, vmem_limit_bytes=vmem_mb * 1024 * 1024)


def _dot(a, b, precision=None):
    return jnp.dot(a, b, precision=precision, preferred_element_type=F32)


def _dot_nt(a, b, precision=None):
    return lax.dot_general(a, b, (((1,), (1,)), ((), ())), precision=precision, preferred_element_type=F32)


def _dot_tn(a, b, precision=None):
    return lax.dot_general(a, b, (((0,), (0,)), ((), ())), precision=precision, preferred_element_type=F32)


NN = ((1,), (0,))
NT = ((1,), (1,))
TN = ((0,), (0,))


def _split(x):
    hi = x.astype(BF16)
    return hi, (x - hi.astype(F32)).astype(BF16)


def _mx(a, b, dims, passes):
    dg = lambda p, q: lax.dot_general(p, q, (dims, ((), ())), preferred_element_type=F32)
    if passes == 1:
        return dg(a.astype(BF16), b.astype(BF16))
    ah, al = _split(a)
    bh, bl = _split(b)
    return dg(ah, bh) + dg(ah, bl) + dg(al, bh)


def _mx_exact_lhs(a_bf16, b):
    bh, bl = _split(b)
    return _dot(a_bf16, bh) + _dot(a_bf16, bl)


def _segsum(x, j):
    hi, lo = _split(x)
    return _dot(hi, j) + _dot(lo, j)


def _shift_rows(z, prev_row, next_row, zero_prev, zero_next):
    tm = z.shape[0]
    rowi = lax.broadcasted_iota(jnp.int32, (tm, 1), 0)
    prev = jnp.where(rowi == 0, prev_row, pltpu.roll(z, 1, 0))
    nxt = jnp.where(rowi == tm - 1, next_row, pltpu.roll(z, tm - 1, 0))
    prev = jnp.where(zero_prev, 0.0, prev)
    nxt = jnp.where(zero_next, 0.0, nxt)
    return prev, nxt


def _mod_kernel(c_ref, w_ref, b_ref, o_ref):
    c = c_ref[...]
    s = c * jax.nn.sigmoid(c)
    o_ref[0] = _dot(s, w_ref[0], HI) + b_ref[0]


def _ada_mod(c_all, w_ada, b_ada):
    depth, d, d3 = w_ada.shape
    tn = 512
    return pl.pallas_call(
        _mod_kernel,
        grid=(depth, d3 // tn),
        in_specs=[pl.BlockSpec((MOD_ROWS, d), lambda l, j: (0, 0)),
                  pl.BlockSpec((1, d, tn), lambda l, j: (l, 0, j)),
                  pl.BlockSpec((1, 1, tn), lambda l, j: (l, 0, j))],
        out_specs=pl.BlockSpec((1, MOD_ROWS, tn), lambda l, j: (l, 0, j)),
        out_shape=jax.ShapeDtypeStruct((depth, MOD_ROWS, d3), F32),
        compiler_params=_cparams(("parallel", "parallel"), 32),
        name="ada_mod",
    )(c_all, w_ada, b_ada.reshape(depth, 1, d3))


def _norm_mod_kernel(x_ref, g_ref, shx_ref, scx_ref, shc_ref, scc_ref, o_ref, *, n_ctx_blocks):
    x = x_ref[0]
    ms = jnp.mean(x * x, axis=-1, keepdims=True)
    y = x * lax.rsqrt(ms + NORM_EPS) * g_ref[...]
    is_ctx = pl.program_id(1) < n_ctx_blocks
    sc = jnp.where(is_ctx, scc_ref[0, 0], scx_ref[0, 0])
    sh = jnp.where(is_ctx, shc_ref[0, 0], shx_ref[0, 0])
    o_ref[0] = (y * (1.0 + sc) + sh).astype(BF16)


def _norm_mod(x_all, g, mod_l, n_ctx):
    bsz, n, d = x_all.shape
    tm = ROW_TILE
    ctx_row = bsz
    return pl.pallas_call(
        functools.partial(_norm_mod_kernel, n_ctx_blocks=n_ctx // tm),
        grid=(bsz, n // tm),
        in_specs=[pl.BlockSpec((1, tm, d), lambda b, j: (b, j, 0)),
                  pl.BlockSpec((1, d), lambda b, j: (0, 0)),
                  pl.BlockSpec((1, 1, 1, d), lambda b, j: (b, 0, 0, 0)),
                  pl.BlockSpec((1, 1, 1, d), lambda b, j: (b, 1, 0, 0)),
                  pl.BlockSpec((1, 1, 1, d), lambda b, j: (ctx_row, 0, 0, 0)),
                  pl.BlockSpec((1, 1, 1, d), lambda b, j: (ctx_row, 1, 0, 0))],
        out_specs=pl.BlockSpec((1, tm, d), lambda b, j: (b, j, 0)),
        out_shape=jax.ShapeDtypeStruct((bsz, n, d), BF16),
        compiler_params=_cparams(("parallel", "parallel"), 32),
        name="norm_mod",
    )(x_all, g.reshape(1, d), mod_l, mod_l, mod_l, mod_l)


def _mm_kernel(a_ref, b_ref, o_ref):
    o_ref[...] = _dot(a_ref[...], b_ref[...]).astype(o_ref.dtype)


def _mm(a, b, tm, tn):
    m, k = a.shape
    n = b.shape[1]
    assert m % tm == 0 and n % tn == 0
    return pl.pallas_call(
        _mm_kernel,
        grid=(n // tn, m // tm),
        in_specs=[pl.BlockSpec((tm, k), lambda j, i: (i, 0)),
                  pl.BlockSpec((k, tn), lambda j, i: (0, j))],
        out_specs=pl.BlockSpec((tm, tn), lambda j, i: (i, j)),
        out_shape=jax.ShapeDtypeStruct((m, n), F32),
        compiler_params=_cparams(("parallel", "parallel"), 48),
        name="in_proj",
    )(a, b)


def _s5_weights(lam_re, lam_im, log_step, b_re, b_im, c_re, c_im, n_scan_steps, nc_lat, nc_ctx):
    t_len = S5_T
    step = jnp.exp(log_step)[..., None]
    th_re, th_im = lam_re * step, lam_im * step

    def cpow(k):
        mag = jnp.exp(th_re[..., None] * k)
        ang = th_im[..., None] * k
        return mag * jnp.cos(ang), mag * jnp.sin(ang)

    lb_re, lb_im = (t[..., 0] for t in cpow(jnp.ones((1,), F32)))
    den = lam_re * lam_re + lam_im * lam_im
    nr = lb_re - 1.0
    co_re = (nr * lam_re + lb_im * lam_im) / den
    co_im = (lb_im * lam_re - nr * lam_im) / den
    bb_re = co_re[..., None] * b_re - co_im[..., None] * b_im
    bb_im = co_re[..., None] * b_im + co_im[..., None] * b_re

    lags = jnp.arange(t_len + 1, dtype=F32)
    pw_re, pw_im = cpow(lags)
    x_re = c_re[..., None] * pw_re[:, :, None] - c_im[..., None] * pw_im[:, :, None]
    x_im = c_re[..., None] * pw_im[:, :, None] + c_im[..., None] * pw_re[:, :, None]
    m_k = (jnp.einsum('dgopk,dgpi->dgkio', x_re, bb_re, precision=HI)
           - jnp.einsum('dgopk,dgpi->dgkio', x_im, bb_im, precision=HI))
    s_idx = jnp.arange(t_len)[:, None]
    t_idx = jnp.arange(t_len)[None, :]
    lag_f = t_idx - s_idx
    kin = []
    for d, lag in enumerate((lag_f, -lag_f)):
        blk = m_k[d][:, jnp.clip(lag, 0, t_len)]
        blk = jnp.where((lag >= 0)[None, :, :, None, None], blk, 0.0)
        kin.append(blk.transpose(0, 1, 3, 2, 4).reshape(S5_G, t_len * S5_H, t_len * S5_H))
    kin = jnp.stack(kin)

    tt = jnp.arange(t_len)
    win, wout = [], []
    for d in range(2):
        e_in = (t_len - 1 - tt) if d == 0 else tt
        e_out = (tt + 1) if d == 0 else (t_len - tt)
        pr, pi = pw_re[d][..., e_in], pw_im[d][..., e_in]
        wr = pr[..., None] * bb_re[d][:, :, None] - pi[..., None] * bb_im[d][:, :, None]
        wi = pr[..., None] * bb_im[d][:, :, None] + pi[..., None] * bb_re[d][:, :, None]
        w = jnp.concatenate([wr, wi], axis=1)
        win.append(w.transpose(0, 2, 3, 1).reshape(S5_G, t_len * S5_H, 2 * S5_P))
        qr, qi = pw_re[d][..., e_out], pw_im[d][..., e_out]
        orr = c_re[d].transpose(0, 2, 1)[:, :, None] * qr[..., None] - c_im[d].transpose(0, 2, 1)[:, :, None] * qi[..., None]
        oii = -(c_re[d].transpose(0, 2, 1)[:, :, None] * qi[..., None] + c_im[d].transpose(0, 2, 1)[:, :, None] * qr[..., None])
        o = jnp.concatenate([orr, oii], axis=1)
        wout.append(o.reshape(S5_G, 2 * S5_P, t_len * S5_H))
    win, wout = jnp.stack(win), jnp.stack(wout)

    sc_re, sc_im = cpow(t_len * (2.0 ** jnp.arange(n_scan_steps, dtype=F32)))
    a1 = jnp.concatenate([sc_re, sc_re], axis=2)
    a2 = jnp.concatenate([-sc_im, sc_im], axis=2)
    pw = jnp.stack([a1, a2], axis=-1).transpose(0, 1, 3, 4, 2).reshape(2, S5_G, 2 * n_scan_steps, 2 * S5_P)
    cr, ci_ = (t[1] for t in cpow(t_len * (nc_lat - 1 - jnp.arange(nc_lat, dtype=F32))))
    tab = jnp.stack([jnp.concatenate([cr, cr], axis=1), jnp.concatenate([-ci_, ci_], axis=1)], axis=1)
    tab = jnp.pad(tab.transpose(0, 1, 3, 2), ((0, 0), (0, 0), (nc_ctx, 0), (0, 0)))
    return kin.astype(BF16), win.astype(BF16), wout.astype(BF16), pw, tab


S5_CB = 16
LANES = 128
S5_GPV = LANES // S5_H
S5_NV = BRANCH_W // LANES


def _lane_block_masks():
    blk = lax.broadcasted_iota(jnp.int32, (1, LANES), 1) // S5_H
    return [blk == m for m in range(S5_GPV)]


def _roll_lanes(x, shift):
    shift %= LANES
    return pltpu.roll(x, shift, 1) if shift else x


def _s5_pack_kernel(u_ref, x_ref):
    masks = _lane_block_masks()
    step = S5_T * S5_NV
    q = [[_roll_lanes(u_ref[0, pl.ds(t * S5_NV + j, S5_CB, stride=step), :], S5_H * t) for j in range(S5_NV)]
         for t in range(S5_T)]
    for j in range(S5_NV):
        for r in range(S5_GPV):
            for half in range(S5_T // S5_GPV):
                z = jnp.zeros((S5_CB, LANES), F32)
                for tp in range(S5_GPV):
                    z = jnp.where(masks[(r + tp) % S5_GPV], q[half * S5_GPV + tp][j], z)
                x_ref[j * S5_GPV + r, 0, :, half * LANES:(half + 1) * LANES] = _roll_lanes(z, -S5_H * r).astype(BF16)


def _s5_unpack_kernel(y_ref, o_ref):
    masks = _lane_block_masks()
    step = S5_T * S5_NV
    for j in range(S5_NV):
        for half in range(S5_T // S5_GPV):
            rl = [_roll_lanes(y_ref[j * S5_GPV + r, 0, :, half * LANES:(half + 1) * LANES], S5_H * r)
                  for r in range(S5_GPV)]
            for tp in range(S5_GPV):
                z = jnp.zeros((S5_CB, LANES), F32)
                for r in range(S5_GPV):
                    z = jnp.where(masks[(r + tp) % S5_GPV], rl[r], z)
                t = half * S5_GPV + tp
                o_ref[0, pl.ds(t * S5_NV + j, S5_CB, stride=step), :] = _roll_lanes(z, -S5_H * tp)


def _s5_kernel(x_ref, kin_ref, win_ref, wout_ref, pw_ref, tab_ref, y_ref, *, bsz, nc, nc_ctx, n_steps):
    rows = bsz * nc
    th = S5_T * S5_H
    sw = 2 * S5_P
    x = x_ref[0].reshape(rows, th)
    cidx = lax.broadcasted_iota(jnp.int32, (bsz, nc, sw), 1).reshape(rows, sw)
    seg_hi = jnp.where(cidx < nc_ctx, nc_ctx, nc)

    def cmul(a1, a2, h):
        return a1 * h + a2 * pltpu.roll(h, S5_P, 1)

    h = _dot(x, win_ref[0, 0])
    for i in range(n_steps):
        d = 1 << i
        a1 = pw_ref[0, 0, 2 * i:2 * i + 1, :]
        a2 = pw_ref[0, 0, 2 * i + 1:2 * i + 2, :]
        h = h + jnp.where(cidx >= d, cmul(a1, a2, pltpu.roll(h, d, 0)), 0.0)
    h_start = jnp.where(cidx >= 1, pltpu.roll(h, 1, 0), 0.0)
    y = _dot(x, kin_ref[0, 0]) + _dot(x, kin_ref[1, 0]) + _dot(h_start.astype(BF16), wout_ref[0, 0])

    h = _dot(x, win_ref[1, 0])
    for i in range(n_steps):
        d = 1 << i
        a1 = pw_ref[1, 0, 2 * i:2 * i + 1, :]
        a2 = pw_ref[1, 0, 2 * i + 1:2 * i + 2, :]
        h = h + jnp.where(cidx + d < seg_hi, cmul(a1, a2, pltpu.roll(h, rows - d, 0)), 0.0)
    h_ctx = jnp.broadcast_to(h.reshape(bsz, nc, sw)[:, 0:1, :], (bsz, nc, sw)).reshape(rows, sw)
    tab1 = jnp.broadcast_to(tab_ref[0, 0][None], (bsz, nc, sw)).reshape(rows, sw)
    tab2 = jnp.broadcast_to(tab_ref[0, 1][None], (bsz, nc, sw)).reshape(rows, sw)
    h_start = jnp.where(cidx + 1 < seg_hi, pltpu.roll(h, rows - 1, 0), 0.0) + cmul(tab1, tab2, h_ctx)
    y = y + _dot(h_start.astype(BF16), wout_ref[1, 0])
    y_ref[0] = y.reshape(bsz, nc, th)


def _s5_mix(u, weights, n_ctx):
    kin, win, wout, pw, tab = weights
    bsz, n, w = u.shape
    nc, nc_ctx = n // S5_T, n_ctx // S5_T
    th = S5_T * S5_H
    n_steps = pw.shape[2] // 2
    assert (1 << n_steps) >= nc and nc % S5_CB == 0 and nc_ctx >= 1
    tok_rows = S5_CB * S5_T * S5_NV
    x = pl.pallas_call(
        _s5_pack_kernel,
        grid=(bsz, nc // S5_CB),
        in_specs=[pl.BlockSpec((1, tok_rows, LANES), lambda b, c: (b, c, 0))],
        out_specs=pl.BlockSpec((S5_G, 1, S5_CB, th), lambda b, c: (0, b, c, 0)),
        out_shape=jax.ShapeDtypeStruct((S5_G, bsz, nc, th), BF16),
        compiler_params=_cparams(("parallel", "parallel"), 32),
        name="s5_pack",
    )(u.reshape(bsz, n * S5_NV, LANES))
    y = pl.pallas_call(
        functools.partial(_s5_kernel, bsz=bsz, nc=nc, nc_ctx=nc_ctx, n_steps=n_steps),
        grid=(S5_G,),
        in_specs=[pl.BlockSpec((1, bsz, nc, th), lambda g: (g, 0, 0, 0)),
                  pl.BlockSpec((2, 1, th, th), lambda g: (0, g, 0, 0)),
                  pl.BlockSpec((2, 1, th, 2 * S5_P), lambda g: (0, g, 0, 0)),
                  pl.BlockSpec((2, 1, 2 * S5_P, th), lambda g: (0, g, 0, 0)),
                  pl.BlockSpec((2, 1, 2 * n_steps, 2 * S5_P), lambda g: (0, g, 0, 0)),
                  pl.BlockSpec((1, 2, nc, 2 * S5_P), lambda g: (g, 0, 0, 0))],
        out_specs=pl.BlockSpec((1, bsz, nc, th), lambda g: (g, 0, 0, 0)),
        out_shape=jax.ShapeDtypeStruct((S5_G, bsz, nc, th), F32),
        compiler_params=_cparams(("parallel",), 40),
        name="s5_chunks",
    )(x, kin, win, wout, pw, tab)
    out = pl.pallas_call(
        _s5_unpack_kernel,
        grid=(bsz, nc // S5_CB),
        in_specs=[pl.BlockSpec((S5_G, 1, S5_CB, th), lambda b, c: (0, b, c, 0))],
        out_specs=pl.BlockSpec((1, tok_rows, LANES), lambda b, c: (b, c, 0)),
        out_shape=jax.ShapeDtypeStruct((bsz, n * S5_NV, LANES), F32),
        compiler_params=_cparams(("parallel", "parallel"), 32),
        name="s5_unpack",
    )(y)
    return out.reshape(bsz, n, w)


def _rope_tables(n_ctx, n_lat):
    quarter = HEAD_DIM // 4
    inv = 1.0 / (ROPE_BASE ** (jnp.arange(quarter, dtype=F32) / quarter))
    t = jnp.arange(n_lat, dtype=jnp.int32)
    ar = (t // GRID_W).astype(F32)[:, None] * inv[None, :]
    ac = (t % GRID_W).astype(F32)[:, None] * inv[None, :]
    cos = jnp.concatenate([jnp.cos(ar), jnp.cos(ar), jnp.cos(ac), jnp.cos(ac)], axis=1)
    sin = jnp.concatenate([-jnp.sin(ar), jnp.sin(ar), -jnp.sin(ac), jnp.sin(ac)], axis=1)
    cos = jnp.concatenate([jnp.ones((n_ctx, HEAD_DIM), F32), cos], axis=0)
    sin = jnp.concatenate([jnp.zeros((n_ctx, HEAD_DIM), F32), sin], axis=0)
    return jnp.concatenate([cos, cos], axis=1), jnp.concatenate([sin, sin], axis=1)


def _qk_prep_kernel(x_ref, cos_ref, sin_ref, qg_ref, kg_ref, j_ref, q_out, k_out, v_out):
    x = x_ref[0]
    wq, wk = ATT_HEADS * HEAD_DIM, ATT_KV_HEADS * HEAD_DIM
    cos, sin = cos_ref[...], sin_ref[...]

    def prep(t, g, w, scale):
        ss = _segsum(t * t, j_ref[0:w, 0:w])
        tn = t * lax.rsqrt(ss * (1.0 / HEAD_DIM) + NORM_EPS) * g
        reps = w // 128
        c = jnp.concatenate([cos] * reps, axis=1) if reps > 1 else cos
        s = jnp.concatenate([sin] * reps, axis=1) if reps > 1 else sin
        lane = lax.broadcasted_iota(jnp.int32, tn.shape, 1)
        first = (lane % 32) < 16
        swapped = jnp.where(first, pltpu.roll(tn, w - 16, 1), pltpu.roll(tn, 16, 1))
        return ((tn * c + swapped * s) * scale).astype(BF16)

    q = prep(x[:, 0:wq], qg_ref[...], wq, HEAD_DIM ** -0.5)
    k = prep(x[:, wq:wq + wk], kg_ref[...], wk, 1.0)
    v = x[:, wq + wk:wq + 2 * wk].astype(BF16)
    for h in range(ATT_HEADS):
        q_out[0, h] = q[:, h * HEAD_DIM:(h + 1) * HEAD_DIM]
    for h in range(ATT_KV_HEADS):
        k_out[0, h] = k[:, h * HEAD_DIM:(h + 1) * HEAD_DIM]
        v_out[0, h] = v[:, h * HEAD_DIM:(h + 1) * HEAD_DIM]


def _qk_prep(qkv, cos, sin, q_g, k_g, seg_j):
    bsz, n, w = qkv.shape
    tm = ROW_TILE
    wq, wk = ATT_HEADS * HEAD_DIM, ATT_KV_HEADS * HEAD_DIM
    return pl.pallas_call(
        _qk_prep_kernel,
        grid=(bsz, n // tm),
        in_specs=[pl.BlockSpec((1, tm, w), lambda b, j: (b, j, 0)),
                  pl.BlockSpec((tm, 128), lambda b, j: (j, 0)),
                  pl.BlockSpec((tm, 128), lambda b, j: (j, 0)),
                  pl.BlockSpec((1, wq), lambda b, j: (0, 0)),
                  pl.BlockSpec((1, wk), lambda b, j: (0, 0)),
                  pl.BlockSpec((BRANCH_W, BRANCH_W), lambda b, j: (0, 0))],
        out_specs=[pl.BlockSpec((1, ATT_HEADS, tm, HEAD_DIM), lambda b, j: (b, 0, j, 0)),
                   pl.BlockSpec((1, ATT_KV_HEADS, tm, HEAD_DIM), lambda b, j: (b, 0, j, 0)),
                   pl.BlockSpec((1, ATT_KV_HEADS, tm, HEAD_DIM), lambda b, j: (b, 0, j, 0))],
        out_shape=[jax.ShapeDtypeStruct((bsz, ATT_HEADS, n, HEAD_DIM), BF16),
                   jax.ShapeDtypeStruct((bsz, ATT_KV_HEADS, n, HEAD_DIM), BF16),
                   jax.ShapeDtypeStruct((bsz, ATT_KV_HEADS, n, HEAD_DIM), BF16)],
        compiler_params=_cparams(("parallel", "parallel"), 32),
        name="qk_prep",
    )(qkv, cos, sin, jnp.tile(q_g, ATT_HEADS).reshape(1, wq), jnp.tile(k_g, ATT_KV_HEADS).reshape(1, wk), seg_j)


def _attn_kernel(sink_ref, q_ref, k_ref, v_ref, o_ref, *, n_ctx, n_lat):
    i = pl.program_id(1)
    n_ctx_blocks = n_ctx // BLOCK
    rows = ATT_REP * BLOCK
    span = 3 * BLOCK
    kv_heads = range(ATT_KV_HEADS)
    rowi = lax.broadcasted_iota(jnp.int32, (rows, 1), 0)
    rep = rowi // BLOCK
    q = [q_ref[0, h * ATT_REP:(h + 1) * ATT_REP].reshape(rows, HEAD_DIM) for h in kv_heads]
    kc = [k_ref[0, h, 0:n_ctx, :] for h in kv_heads]
    vc = [v_ref[0, h, 0:n_ctx, :] for h in kv_heads]
    s_ctx = [_dot_nt(q[h], kc[h]) for h in kv_heads]
    sink = []
    for h in kv_heads:
        s = jnp.zeros((rows, 1), F32)
        for r in range(ATT_REP):
            s = jnp.where(rep == r, sink_ref[h * ATT_REP + r], s)
        sink.append(s)

    def store(o):
        o_ref[0] = jnp.concatenate([o[h][r * BLOCK:(r + 1) * BLOCK] for h in kv_heads for r in range(ATT_REP)], axis=1)

    @pl.when(i < n_ctx_blocks)
    def _():
        m = [jnp.maximum(jnp.max(s_ctx[h], axis=1, keepdims=True), sink[h]) for h in kv_heads]
        p = [jnp.exp(s_ctx[h] - m[h]) for h in kv_heads]
        den = [jnp.sum(p[h], axis=1, keepdims=True) + jnp.exp(sink[h] - m[h]) for h in kv_heads]
        store([_dot(p[h].astype(BF16), vc[h]) / den[h] for h in kv_heads])

    @pl.when(i >= n_ctx_blocks)
    def _():
        il = i - n_ctx_blocks
        start = pl.multiple_of(jnp.clip((il - 1) * BLOCK, 0, n_lat - span), BLOCK)
        qpos = il * BLOCK + rowi % BLOCK
        kpos = start + lax.broadcasted_iota(jnp.int32, (rows, span), 1)
        band = jnp.abs(qpos - kpos) <= WINDOW
        kw = [k_ref[0, h, pl.ds(n_ctx + start, span), :] for h in kv_heads]
        vw = [v_ref[0, h, pl.ds(n_ctx + start, span), :] for h in kv_heads]
        s_loc = [jnp.where(band, _dot_nt(q[h], kw[h]), NEG_INF) for h in kv_heads]
        m = [jnp.maximum(jnp.maximum(jnp.max(s_loc[h], axis=1, keepdims=True), jnp.max(s_ctx[h], axis=1, keepdims=True)),
                         sink[h]) for h in kv_heads]
        p_loc = [jnp.exp(s_loc[h] - m[h]) for h in kv_heads]
        p_ctx = [jnp.exp(s_ctx[h] - m[h]) for h in kv_heads]
        den = [jnp.sum(p_loc[h], axis=1, keepdims=True) + jnp.sum(p_ctx[h], axis=1, keepdims=True)
               + jnp.exp(sink[h] - m[h]) for h in kv_heads]
        store([(_dot(p_loc[h].astype(BF16), vw[h]) + _dot(p_ctx[h].astype(BF16), vc[h])) / den[h] for h in kv_heads])


def _attn(q, k, v, sink, n_ctx):
    bsz, _, n, _ = q.shape
    n_lat = n - n_ctx
    assert n_lat >= 3 * BLOCK and n_ctx % BLOCK == 0
    return pl.pallas_call(
        functools.partial(_attn_kernel, n_ctx=n_ctx, n_lat=n_lat),
        grid=(bsz, n // BLOCK),
        in_specs=[pl.BlockSpec(memory_space=pltpu.SMEM),
                  pl.BlockSpec((1, ATT_HEADS, BLOCK, HEAD_DIM), lambda b, i: (b, 0, i, 0)),
                  pl.BlockSpec((1, ATT_KV_HEADS, n, HEAD_DIM), lambda b, i: (b, 0, 0, 0)),
                  pl.BlockSpec((1, ATT_KV_HEADS, n, HEAD_DIM), lambda b, i: (b, 0, 0, 0))],
        out_specs=pl.BlockSpec((1, BLOCK, ATT_HEADS * HEAD_DIM), lambda b, i: (b, i, 0)),
        out_shape=jax.ShapeDtypeStruct((bsz, n, ATT_HEADS * HEAD_DIM), F32),
        compiler_params=_cparams(("parallel", "parallel"), 32),
        name="window_attn",
    )(sink, q, k, v)


RW_FEATS = 6
RW_PASSES_SCORE = 1
RW_PASSES_INV = 1
RW_PASSES_MIX = 1
RW_PASSES_STATE = 1
RW_PASSES_OUT = 1


def _softplus(y):
    return jnp.maximum(y, 0.0) + jnp.log(1.0 + jnp.exp(-jnp.abs(y)))


def _rw_feat_kernel(x_ref, p_ref, n_ref, mup_ref, mun_ref, w2_ref, a2_ref, w0_ref, a0_ref, kk_ref, ka_ref, rk_ref,
                    j_ref, f_out, v_out, gc_out, bv_out, *, tm, n_ctx, n_tot):
    w = BRANCH_W
    z = x_ref[0]
    rowi = lax.broadcasted_iota(jnp.int32, (tm, 1), 0)
    pos = pl.program_id(1) * tm + rowi
    prev, nxt = _shift_rows(z, p_ref[0, 7:8, :], n_ref[0, 0:1, :],
                            (pos == 0) | (pos == n_ctx), (pos == n_ctx - 1) | (pos == n_tot - 1))
    zz = z + mup_ref[...] * (prev - z) + mun_ref[...] * (nxt - z)
    r, k, v, lo = zz[:, 0:w], zz[:, w:2 * w], zz[:, 2 * w:3 * w], zz[:, 3 * w:]
    seg_j = j_ref[...]
    kk = k * kk_ref[...]
    kk = kk * lax.rsqrt(_segsum(kk * kk, seg_j) + 1e-12)
    w_log = -_softplus(-(_mx(jnp.tanh(lo), w2_ref[...], NN, 3) + w0_ref[...])) - 0.5
    log_decay = -jnp.exp(w_log)
    a = jax.nn.sigmoid(_mx(lo, a2_ref[...], NN, 3) + a0_ref[...])
    ri = lax.broadcasted_iota(jnp.int32, (tm, tm), 0)
    ci = lax.broadcasted_iota(jnp.int32, (tm, tm), 1)
    same = (ri // RW_C) == (ci // RW_C)
    tot_m = jnp.where(same, 1.0, 0.0).astype(BF16)
    bonus = jnp.zeros((tm, w), F32)
    for d in range(2):
        ld = log_decay[:, d * w:(d + 1) * w]
        ad = a[:, d * w:(d + 1) * w]
        kd = k * (1.0 + (ad - 1.0) * ka_ref[...])
        b = kk * ad
        bonus = bonus + r * kd * rk_ref[...]
        tri = jnp.where(same & ((ci <= ri) if d == 0 else (ci >= ri)), 1.0, 0.0).astype(BF16)
        cum = _mx_exact_lhs(tri, ld)
        tot = _mx_exact_lhs(tot_m, ld)
        e_neg = jnp.exp(-cum)
        e_hat = jnp.exp(tot - cum)
        feats = (kk * jnp.exp(cum - ld), b * e_neg, kd * e_neg, r * jnp.exp(cum), kd * e_hat, b * e_hat)
        for q in range(RW_FEATS):
            for h in range(RW_HEADS):
                f_out[0, d, q, h] = feats[q][:, h * RW_N:(h + 1) * RW_N].astype(BF16)
        g = jnp.exp(tot)
        gc_out[0, d, 0] = jnp.concatenate([g[c * RW_C:c * RW_C + 1] for c in range(tm // RW_C)], axis=0)
    for h in range(RW_HEADS):
        v_out[0, h] = v[:, h * RW_N:(h + 1) * RW_N].astype(BF16)
    bv_out[0] = _segsum(bonus, seg_j) * v


def _rw_features(rw, params, seg_j, n_ctx):
    mu_prev, mu_next, w0, w2, a0, a2, k_k, k_a, r_k = params
    bsz, n, wz = rw.shape
    tm = ROW_TILE
    w = BRANCH_W
    nb8 = n // 8
    lw = 4 * RW_LORA
    w2p = jnp.zeros((lw, 2 * w), F32)
    a2p = jnp.zeros((lw, 2 * w), F32)
    for d in range(2):
        w2p = w2p.at[d * RW_LORA:(d + 1) * RW_LORA, d * w:(d + 1) * w].set(w2[d])
        a2p = a2p.at[(2 + d) * RW_LORA:(3 + d) * RW_LORA, d * w:(d + 1) * w].set(a2[d])
    vec = lambda t: t.reshape(1, -1)
    full = lambda shape: pl.BlockSpec(shape, lambda b, j: (0,) * len(shape))
    gpt = tm // RW_C
    return pl.pallas_call(
        functools.partial(_rw_feat_kernel, tm=tm, n_ctx=n_ctx, n_tot=n),
        grid=(bsz, n // tm),
        in_specs=[pl.BlockSpec((1, tm, wz), lambda b, j: (b, j, 0)),
                  pl.BlockSpec((1, 8, wz), lambda b, j: (b, jnp.maximum(j * (tm // 8) - 1, 0), 0)),
                  pl.BlockSpec((1, 8, wz), lambda b, j: (b, jnp.minimum((j + 1) * (tm // 8), nb8 - 1), 0)),
                  full((1, wz)), full((1, wz)), full((lw, 2 * w)), full((lw, 2 * w)),
                  full((1, 2 * w)), full((1, 2 * w)), full((1, w)), full((1, w)), full((1, w)), full((w, w))],
        out_specs=[pl.BlockSpec((1, 2, RW_FEATS, RW_HEADS, tm, RW_N), lambda b, j: (b, 0, 0, 0, j, 0)),
                   pl.BlockSpec((1, RW_HEADS, tm, RW_N), lambda b, j: (b, 0, j, 0)),
                   pl.BlockSpec((1, 2, 1, gpt, w), lambda b, j: (b, 0, j, 0, 0)),
                   pl.BlockSpec((1, tm, w), lambda b, j: (b, j, 0))],
        out_shape=[jax.ShapeDtypeStruct((bsz, 2, RW_FEATS, RW_HEADS, n, RW_N), BF16),
                   jax.ShapeDtypeStruct((bsz, RW_HEADS, n, RW_N), BF16),
                   jax.ShapeDtypeStruct((bsz, 2, n // tm, gpt, w), F32),
                   jax.ShapeDtypeStruct((bsz, n, w), F32)],
        compiler_params=_cparams(("parallel", "parallel"), 56),
        name="rwkv_features",
    )(rw, rw, rw, vec(mu_prev), vec(mu_next), w2p, a2p, vec(w0), vec(a0), vec(k_k), vec(k_a), vec(r_k), seg_j)


def _tri_inverse(a_list, ri, ci):
    mm = lambda ps, qs: [_mx(p, q, NN, RW_PASSES_INV) for p, q in zip(ps, qs)]
    eye = jnp.where(ri == ci, 1.0, 0.0)
    x = [jnp.where((ri // 8) == (ci // 8), -a, 0.0) for a in a_list]
    p = [eye + t for t in x]
    x2 = mm(x, x)
    p = [s + t for s, t in zip(p, mm(p, x2))]
    x4 = mm(x2, x2)
    p = [s + t for s, t in zip(p, mm(p, x4))]
    blk = 8
    while blk < RW_C:
        off = ((ri // (2 * blk)) == (ci // (2 * blk))) & ((ri // blk) != (ci // blk))
        e = [jnp.where(off, a, 0.0) for a in a_list]
        p = [s - t for s, t in zip(p, mm(mm(p, e), p))]
        blk *= 2
    return p


def _rw_chunk_kernel(ff_ref, fb_ref, vf_ref, vb_ref, gf_ref, gb_ref, yf_ref, yb_ref, state, *, nch, nch_ctx):
    s = pl.program_id(1)
    rvs = jnp.where(s < nch_ctx, nch_ctx - 1 - s, nch + nch_ctx - 1 - s)
    gpt = gf_ref.shape[3]
    nk = RW_N

    @pl.when(s == 0)
    def _():
        state[...] = jnp.zeros_like(state)

    ri = lax.broadcasted_iota(jnp.int32, (RW_C, RW_C), 0)
    ci = lax.broadcasted_iota(jnp.int32, (RW_C, RW_C), 1)
    eye = ri == ci
    before = ((ci < ri), (ci > ri))
    upto = tuple(m | eye for m in before)
    f_refs, v_refs = (ff_ref, fb_ref), (vf_ref, vb_ref)
    g_all = [gf_ref[0, 0, 0, pl.ds(s % gpt, 1), :], gb_ref[0, 0, 0, pl.ds(rvs % gpt, 1), :]]
    ent = [(d, h) for d in range(2) for h in range(RW_HEADS)]
    idx = range(len(ent))
    kap, bet, kt, rt, kh, bh = ([f_refs[d][0, 0, q, h] for d, h in ent] for q in range(RW_FEATS))
    v = [v_refs[d][0, h] for d, h in ent]
    kr = [jnp.concatenate([kap[i], rt[i]], axis=0) for i in idx]
    sb = [_mx(kr[i], bet[i], NT, RW_PASSES_SCORE) for i in idx]
    sk = [_mx(kr[i], kt[i], NT, RW_PASSES_SCORE) for i in idx]
    a_ab = [jnp.where(before[ent[i][0]], sb[i][0:RW_C], 0.0) for i in idx]
    a_qb = [jnp.where(upto[ent[i][0]], sb[i][RW_C:], 0.0) for i in idx]
    a_k = [jnp.concatenate([jnp.where(before[ent[i][0]], sk[i][0:RW_C], 0.0),
                            jnp.where(upto[ent[i][0]], sk[i][RW_C:], 0.0)], axis=0) for i in idx]
    akv = [_mx(a_k[i], v[i], NN, RW_PASSES_MIX) for i in idx]
    vk = [_mx(v[i], kh[i], TN, RW_PASSES_MIX) for i in idx]
    t_inv = _tri_inverse(a_ab, ri, ci)
    wu = [_mx(t_inv[i], jnp.concatenate([kap[i].astype(F32), akv[i][0:RW_C]], axis=1), NN, RW_PASSES_MIX)
          for i in idx]
    wub = [_mx(wu[i], bh[i], TN, RW_PASSES_MIX) for i in idx]
    aq = [_mx(a_qb[i], wu[i], NN, RW_PASSES_MIX) for i in idx]
    s0 = [state[d, h] for d, h in ent]
    out = [_mx(rt[i] - aq[i][:, 0:nk], s0[i], NT, RW_PASSES_OUT) + (akv[i][RW_C:] - aq[i][:, nk:]) for i in idx]
    m_c = [jnp.where(eye, g_all[d][:, h * nk:(h + 1) * nk], 0.0) - wub[i][0:nk] for i, (d, h) in enumerate(ent)]
    s1 = [_mx(s0[i], m_c[i], NN, RW_PASSES_STATE) + (vk[i] - wub[i][nk:]) for i in idx]
    for i, (d, h) in enumerate(ent):
        state[d, h] = s1[i]
    yf_ref[0] = jnp.concatenate(out[0:RW_HEADS], axis=1)
    yb_ref[0] = jnp.concatenate(out[RW_HEADS:], axis=1)


def _rw_chunks(feats, v, gc, n_ctx):
    bsz, _, _, _, n, _ = feats.shape
    nch, nch_ctx = n // RW_C, n_ctx // RW_C
    gpt = gc.shape[3]

    def rv(s):
        return jnp.where(s < nch_ctx, nch_ctx - 1 - s, nch + nch_ctx - 1 - s)

    ident = lambda s: s
    f_spec = lambda d, f: pl.BlockSpec((1, 1, RW_FEATS, RW_HEADS, RW_C, RW_N), lambda b, s: (b, d, 0, 0, f(s), 0))
    v_spec = lambda f: pl.BlockSpec((1, RW_HEADS, RW_C, RW_N), lambda b, s: (b, 0, f(s), 0))
    g_spec = lambda d, f: pl.BlockSpec((1, 1, 1, gpt, BRANCH_W), lambda b, s: (b, d, f(s) // gpt, 0, 0))
    return pl.pallas_call(
        functools.partial(_rw_chunk_kernel, nch=nch, nch_ctx=nch_ctx),
        grid=(bsz, nch),
        in_specs=[f_spec(0, ident), f_spec(1, rv), v_spec(ident), v_spec(rv), g_spec(0, ident), g_spec(1, rv)],
        out_specs=[pl.BlockSpec((1, RW_C, BRANCH_W), lambda b, s: (b, s, 0)),
                   pl.BlockSpec((1, RW_C, BRANCH_W), lambda b, s: (b, rv(s), 0))],
        out_shape=[jax.ShapeDtypeStruct((bsz, n, BRANCH_W), F32), jax.ShapeDtypeStruct((bsz, n, BRANCH_W), F32)],
        scratch_shapes=[pltpu.VMEM((2, RW_HEADS, RW_N, RW_N), F32)],
        compiler_params=_cparams(("parallel", "arbitrary"), 32),
        name="rwkv_chunks",
    )(feats, feats, v, v, gc, gc)


def _conv3_kernel(x_ref, p_ref, n_ref, w_ref, b_ref, o_ref, *, tm, n_blocks):
    j = pl.program_id(1)
    z = x_ref[0]
    rowi = lax.broadcasted_iota(jnp.int32, (tm, 1), 0)
    prev, nxt = _shift_rows(z, p_ref[0, 7:8, :], n_ref[0, 0:1, :],
                            (rowi == 0) & (j == 0), (rowi == tm - 1) & (j == n_blocks - 1))
    o_ref[0] = prev * w_ref[0:1, :] + z * w_ref[1:2, :] + nxt * w_ref[2:3, :] + b_ref[...]


def _hy_conv3(hy, conv_w, conv_b, seg_start, seg_len):
    bsz, _, w = hy.shape
    tm = ROW_TILE
    off, nb = seg_start // tm, seg_len // tm
    off8, nb8 = seg_start // 8, seg_len // 8
    return pl.pallas_call(
        functools.partial(_conv3_kernel, tm=tm, n_blocks=nb),
        grid=(bsz, nb),
        in_specs=[pl.BlockSpec((1, tm, w), lambda b, j: (b, off + j, 0)),
                  pl.BlockSpec((1, 8, w), lambda b, j: (b, off8 + jnp.maximum(j * (tm // 8) - 1, 0), 0)),
                  pl.BlockSpec((1, 8, w), lambda b, j: (b, off8 + jnp.minimum((j + 1) * (tm // 8), nb8 - 1), 0)),
                  pl.BlockSpec((3, w), lambda b, j: (0, 0)),
                  pl.BlockSpec((1, w), lambda b, j: (0, 0))],
        out_specs=pl.BlockSpec((1, tm, w), lambda b, j: (b, j, 0)),
        out_shape=jax.ShapeDtypeStruct((bsz, seg_len, w), F32),
        compiler_params=_cparams(("parallel", "parallel"), 32),
        name="hyena_conv3",
    )(hy, hy, hy, conv_w, conv_b.reshape(1, w))


def _hy_embedding(n):
    t = jnp.linspace(0.0, 1.0, n, dtype=F32)[:, None]
    ang = 2.0 * math.pi * jnp.arange(n, dtype=F32)[:, None] / n
    bands = jnp.linspace(1e-4, HY_BANDS - 1, HY_BANDS, dtype=F32)[None, :]
    z = jnp.concatenate([t, jnp.cos(bands * ang), -jnp.sin(bands * ang)], axis=-1)
    return jnp.pad(z, ((0, 0), (0, HY_FFN - HY_EMB)))


def _hy_filter_kernel(e_ref, w1_ref, b1_ref, f1_ref, w2_ref, b2_ref, f2_ref, w3_ref, dl_ref, o_ref):
    e = e_ref[...]
    h = jnp.sin(f1_ref[...] * (_dot(e, w1_ref[...], HI) + b1_ref[...]))
    h = jnp.sin(f2_ref[...] * (_dot(h, w2_ref[...], HI) + b2_ref[...]))
    o_ref[...] = _dot(h, w3_ref[...], HI) * jnp.exp(-e[:, 0:1] * dl_ref[...])


def _hy_filters(n, w1, b1, f1, w2, b2, f2, w3):
    tm = min(ROW_TILE, n)
    wf = w3.shape[1]
    emb = _hy_embedding(n)
    w1p = jnp.pad(w1, ((0, HY_FFN - HY_EMB), (0, 0)))
    deltas = jnp.abs(jnp.linspace(math.log(HY_TARGET) / HY_SLOW_PCT, math.log(HY_TARGET) / HY_FAST_PCT,
                                  BRANCH_W, dtype=F32))
    dl = jnp.tile(deltas, wf // BRANCH_W).reshape(1, wf)
    vec = lambda t: t.reshape(1, -1)
    full = lambda shape: pl.BlockSpec(shape, lambda i: (0,) * len(shape))
    return pl.pallas_call(
        _hy_filter_kernel,
        grid=(n // tm,),
        in_specs=[pl.BlockSpec((tm, HY_FFN), lambda i: (i, 0)),
                  full((HY_FFN, HY_FFN)), full((1, HY_FFN)), full((1, HY_FFN)),
                  full((HY_FFN, HY_FFN)), full((1, HY_FFN)), full((1, HY_FFN)),
                  full((HY_FFN, wf)), full((1, wf))],
        out_specs=pl.BlockSpec((tm, wf), lambda i: (i, 0)),
        out_shape=jax.ShapeDtypeStruct((n, wf), F32),
        compiler_params=_cparams(("parallel",), 32),
        name="hyena_filters",
    )(emb, w1p, vec(b1), vec(f1), w2, vec(b2), vec(f2), w3, dl)


def _dft_mats(n):
    k = jnp.arange(n, dtype=jnp.int32)
    ang = (math.pi / n) * ((k[:, None] * k[None, :]) % (2 * n)).astype(F32)
    alt = jnp.where(k % 2 == 0, 1.0, -1.0).astype(F32)
    cos, sin = jnp.cos(ang), jnp.sin(ang)
    fwd = jnp.concatenate([cos, (-sin).at[0].set(alt)], axis=0)
    wk = jnp.where(k == 0, 1.0, 2.0).astype(F32)[None, :]
    inv = jnp.concatenate([wk * cos, (-2.0 * sin).at[:, 0].set(alt)], axis=1) / (2 * n)
    return fwd.astype(BF16), inv.astype(BF16)


def _hy_spec_kernel(fc_ref, fs_ref, h_ref, g_ref, *, tk):
    h = h_ref[...]
    n, w2 = h.shape
    w = w2 // 2
    rowi = lax.broadcasted_iota(jnp.int32, (n, w2), 0)
    coli = lax.broadcasted_iota(jnp.int32, (n, w2), 1)
    h = jnp.where((rowi == 0) & (coli >= w), 0.0, h)
    hi = h.astype(BF16)
    lo = (h - hi.astype(F32)).astype(BF16)
    sr = _dot(fc_ref[...], hi) + _dot(fc_ref[...], lo)
    si = _dot(fs_ref[...], hi) + _dot(fs_ref[...], lo)
    krow = pl.program_id(1) * tk + lax.broadcasted_iota(jnp.int32, (tk, 1), 0)
    g_ref[0, 0] = sr[:, 0:w] + sr[:, w:w2]
    g_ref[0, 1] = si[:, 0:w] + jnp.where(krow == 0, 1.0, -1.0) * si[:, w:w2]


def _hy_spectrum(filt, fwd, n):
    w = BRANCH_W
    tk = min(ROW_TILE, n)
    nk = n // tk
    return pl.pallas_call(
        functools.partial(_hy_spec_kernel, tk=tk),
        grid=(2, nk),
        in_specs=[pl.BlockSpec((tk, n), lambda j, kt: (kt, 0)),
                  pl.BlockSpec((tk, n), lambda j, kt: (nk + kt, 0)),
                  pl.BlockSpec((n, 2 * w), lambda j, kt: (0, j))],
        out_specs=pl.BlockSpec((1, 2, tk, w), lambda j, kt: (j, 0, kt, 0)),
        out_shape=jax.ShapeDtypeStruct((2, 2, n, w), F32),
        compiler_params=_cparams(("parallel", "parallel"), 48),
        name="hyena_filter_spectrum",
    )(fwd, fwd, filt)


HY_BATCH = 2
HY_TILE = 512


def _hy_fwd_kernel(fc_ref, fs_ref, u_ref, g_ref, z_ref):
    w = BRANCH_W
    u = jnp.concatenate([u_ref[i] for i in range(HY_BATCH)], axis=1).astype(BF16)
    ur = _dot(fc_ref[...], u)
    ui = _dot(fs_ref[...], u)
    gr = jnp.concatenate([g_ref[0]] * HY_BATCH, axis=1)
    gi = jnp.concatenate([g_ref[1]] * HY_BATCH, axis=1)
    tk = ur.shape[0]
    first = (pl.program_id(1) == 0) & (lax.broadcasted_iota(jnp.int32, (tk, 1), 0) == 0)
    zr = jnp.where(first, ur * gr, ur * gr - ui * gi).astype(BF16)
    zi = jnp.where(first, ui * gi, ur * gi + ui * gr).astype(BF16)
    for i in range(HY_BATCH):
        z_ref[i, 0] = zr[:, i * w:(i + 1) * w]
        z_ref[i, 1] = zi[:, i * w:(i + 1) * w]


def _hy_inv_kernel(gi_ref, z_ref, u_ref, x_ref, skip_ref, o_ref):
    w = BRANCH_W
    y = _dot(gi_ref[...], jnp.concatenate([z_ref[i] for i in range(HY_BATCH)], axis=1))
    for i in range(HY_BATCH):
        o_ref[i] = x_ref[i] * (y[:, i * w:(i + 1) * w] + skip_ref[...] * u_ref[i])


def _hy_long_conv(src, src_col, gate_src, gate_col, spec_j, skip_j, fwd, inv):
    bsz, n, _ = src.shape
    w = BRANCH_W
    hb = HY_BATCH
    assert bsz % hb == 0
    tk = min(HY_TILE, n)
    nk = n // tk
    z = pl.pallas_call(
        _hy_fwd_kernel,
        grid=(bsz // hb, nk),
        in_specs=[pl.BlockSpec((tk, n), lambda b, kt: (kt, 0)),
                  pl.BlockSpec((tk, n), lambda b, kt: (nk + kt, 0)),
                  pl.BlockSpec((hb, n, w), lambda b, kt: (b, 0, src_col)),
                  pl.BlockSpec((2, tk, w), lambda b, kt: (0, kt, 0))],
        out_specs=pl.BlockSpec((hb, 2, tk, w), lambda b, kt: (b, 0, kt, 0)),
        out_shape=jax.ShapeDtypeStruct((bsz, 2, n, w), BF16),
        compiler_params=_cparams(("parallel", "parallel"), 56),
        name="hyena_dft",
    )(fwd, fwd, src, spec_j)
    z = z.reshape(bsz, 2 * n, w)
    return pl.pallas_call(
        _hy_inv_kernel,
        grid=(bsz // hb, nk),
        in_specs=[pl.BlockSpec((tk, 2 * n), lambda b, t: (t, 0)),
                  pl.BlockSpec((hb, 2 * n, w), lambda b, t: (b, 0, 0)),
                  pl.BlockSpec((hb, tk, w), lambda b, t: (b, t, src_col)),
                  pl.BlockSpec((hb, tk, w), lambda b, t: (b, t, gate_col)),
                  pl.BlockSpec((1, w), lambda b, t: (0, 0))],
        out_specs=pl.BlockSpec((hb, tk, w), lambda b, t: (b, t, 0)),
        out_shape=jax.ShapeDtypeStruct((bsz, n, w), F32),
        compiler_params=_cparams(("parallel", "parallel"), 56),
        name="hyena_idft",
    )(inv, z, src, gate_src, skip_j.reshape(1, w))


def _hyena_segment(hy, seg_start, seg_len, conv_w, conv_b, ffn, skip):
    z = _hy_conv3(hy, conv_w, conv_b, seg_start, seg_len)
    filt = _hy_filters(seg_len, *ffn)
    fwd, inv = _dft_mats(seg_len)
    spec = _hy_spectrum(filt, fwd, seg_len)
    y1 = _hy_long_conv(z, 0, z, 1, spec[0], skip[0], fwd, inv)
    return _hy_long_conv(y1, 0, z, 2, spec[1], skip[1], fwd, inv)


def _merge_kernel(x_ref, gx_ref, gc_ref, s5u_ref, s5y_ref, d_ref, gw_ref, gb_ref, att_ref,
                  rwf_ref, rwb_ref, bv_ref, lng_ref, lnb_ref, hyl_ref, hyc_ref, gate_ref, bg_ref, wo_ref, j_ref,
                  o_ref, *, n_ctx_blocks, blk_off):
    w = BRANCH_W
    is_ctx = (pl.program_id(1) + blk_off) < n_ctx_blocks

    def rms(y, g):
        return y * lax.rsqrt(jnp.mean(y * y, axis=-1, keepdims=True) + NORM_EPS) * g

    ys = d_ref[...] * s5u_ref[0] + s5y_ref[0]
    ys = 0.5 * ys * (1.0 + lax.erf(ys * (2.0 ** -0.5)))
    ys = ys * jax.nn.sigmoid(_dot(ys.astype(BF16), gw_ref[...]) + gb_ref[...])
    ys = rms(ys, bg_ref[0:1, :])
    ya = rms(att_ref[0], bg_ref[1:2, :])
    seg_j = j_ref[...]
    yr = rwf_ref[0] + rwb_ref[0]
    mu = _segsum(yr, seg_j) * (1.0 / RW_N)
    yc = yr - mu
    var = _segsum(yc * yc, seg_j) * (1.0 / RW_N)
    yr = yc * lax.rsqrt(var + RW_LN_EPS) * lng_ref[...] + lnb_ref[...] + bv_ref[0]
    yh = rms(jnp.where(is_ctx, hyc_ref[0], hyl_ref[0]), bg_ref[2:3, :])
    gp = gate_ref[0]
    sg = gp * jax.nn.sigmoid(gp)
    acc = _dot((ys * sg[:, 0:w]).astype(BF16), wo_ref[0:w, :])
    acc += _dot((ya * sg[:, w:2 * w]).astype(BF16), wo_ref[w:2 * w, :])
    acc += _dot((yr * sg[:, 2 * w:3 * w]).astype(BF16), wo_ref[2 * w:3 * w, :])
    acc += _dot((yh * sg[:, 3 * w:4 * w]).astype(BF16), wo_ref[3 * w:4 * w, :])
    gate = jnp.where(is_ctx, gc_ref[0, 0], gx_ref[0, 0])
    o_ref[0] = x_ref[0] + gate * acc


def _merge(x_all, mod_l, s5u, s5y, d_skip, glu_w, glu_b, att, rwf, rwb, bv, ln_g, ln_b, hy_lat, hy_ctx,
           gate_pre, branch_g, w_out, seg_j, n_ctx, with_ctx):
    bsz, n, d = x_all.shape
    w = BRANCH_W
    tm = ROW_TILE
    ncb = n_ctx // tm
    off = 0 if with_ctx else ncb
    nb = n // tm - off
    nlb = (n - n_ctx) // tm
    ctx_row = bsz
    row = lambda width: pl.BlockSpec((1, tm, width), lambda b, j: (b, j + off, 0))
    full = lambda shape: pl.BlockSpec(shape, lambda b, j: (0,) * len(shape))
    if hy_ctx is None:
        hy_ctx = hy_lat
        hyc_spec = pl.BlockSpec((1, tm, w), lambda b, j: (b, 0, 0))
    else:
        hyc_spec = pl.BlockSpec((1, tm, w), lambda b, j: (b, jnp.minimum(j + off, ncb - 1), 0))
    vec = lambda t: t.reshape(1, -1)
    return pl.pallas_call(
        functools.partial(_merge_kernel, n_ctx_blocks=ncb, blk_off=off),
        grid=(bsz, nb),
        in_specs=[row(d),
                  pl.BlockSpec((1, 1, 1, d), lambda b, j: (b, 2, 0, 0)),
                  pl.BlockSpec((1, 1, 1, d), lambda b, j: (ctx_row, 2, 0, 0)),
                  row(w), row(w), full((1, w)), full((w, w)), full((1, w)),
                  row(w), row(w), row(w), row(w), full((1, w)), full((1, w)),
                  pl.BlockSpec((1, tm, w), lambda b, j: (b, jnp.clip(j + off - ncb, 0, nlb - 1), 0)),
                  hyc_spec, row(4 * w), full((3, w)), full((4 * w, d)), full((w, w))],
        out_specs=pl.BlockSpec((1, tm, d), lambda b, j: (b, j, 0)),
        out_shape=jax.ShapeDtypeStruct((bsz, nb * tm, d), F32),
        compiler_params=_cparams(("parallel", "parallel"), 56),
        name="merge_out_proj",
    )(x_all, mod_l, mod_l, s5u, s5y, vec(d_skip), glu_w.astype(BF16), vec(glu_b), att,
      rwf, rwb, bv, vec(ln_g), vec(ln_b), hy_lat, hy_ctx, gate_pre, branch_g, w_out, seg_j)


def kernel(x, c, ctx, c_ctx, norm_g, w_ada, b_ada, w_in, w_out, branch_g, s5_lam_re, s5_lam_im, s5_log_step, s5_b_re, s5_b_im, s5_c_re, s5_c_im, s5_d, s5_glu_w, s5_glu_b, att_q_g, att_k_g, att_sink, rw_mu_prev, rw_mu_next, rw_w0, rw_w2, rw_a0, rw_a2, rw_k_k, rw_k_a, rw_r_k, rw_ln_g, rw_ln_b, hy_conv_w, hy_conv_b, hy_w1, hy_b1, hy_f1, hy_w2, hy_b2, hy_f2, hy_w3, hy_skip):
    bsz, n_lat, d = x.shape
    n_ctx = ctx.shape[1]
    depth = w_ada.shape[0]
    n = n_ctx + n_lat
    assert bsz + 1 <= MOD_ROWS and n_ctx % ROW_TILE == 0 and n_lat % ROW_TILE == 0

    c_all = jnp.zeros((MOD_ROWS, d), F32).at[0:bsz].set(c).at[bsz].set(c_ctx)
    mod = _ada_mod(c_all, w_ada, b_ada).reshape(depth, MOD_ROWS, 3, 1, d)
    w_in_b = w_in.astype(BF16)
    w_out_b = w_out.astype(BF16)
    rope_cos, rope_sin = _rope_tables(n_ctx, n_lat)
    lane = jnp.arange(BRANCH_W)
    seg_j = (lane[:, None] // HEAD_DIM == lane[None, :] // HEAD_DIM).astype(BF16)
    n_scan_steps = max(1, math.ceil(math.log2(n // S5_T)))
    slabs = (P_S5, P_QKV, P_RW, P_HY, P_GATE)
    tiles = (P_S5, P_QKV, P_RW, P_HY // 2, P_GATE // 2)
    offs = [sum(slabs[:i]) for i in range(len(slabs))]

    x_all = jnp.concatenate([ctx, x], axis=1)
    for l in range(depth):
        with_ctx = l < depth - 1
        h = _norm_mod(x_all, norm_g[l], mod[l], n_ctx).reshape(bsz * n, d)
        s5u, qkv, rw, hy, gate_pre = (
            _mm(h, w_in_b[l][:, o:o + wd], 1024, tn).reshape(bsz, n, wd) for o, wd, tn in zip(offs, slabs, tiles))

        s5_w = _s5_weights(s5_lam_re[l], s5_lam_im[l], s5_log_step[l], s5_b_re[l], s5_b_im[l],
                           s5_c_re[l], s5_c_im[l], n_scan_steps, n_lat // S5_T, n_ctx // S5_T)
        s5y = _s5_mix(s5u, s5_w, n_ctx)

        qh, kh, vh = _qk_prep(qkv, rope_cos, rope_sin, att_q_g[l], att_k_g[l], seg_j)
        att = _attn(qh, kh, vh, att_sink[l], n_ctx)

        feats, rv, gc, bv = _rw_features(
            rw, (rw_mu_prev[l], rw_mu_next[l], rw_w0[l], rw_w2[l], rw_a0[l], rw_a2[l], rw_k_k[l], rw_k_a[l],
                 rw_r_k[l].reshape(-1)), seg_j, n_ctx)
        rwf, rwb = _rw_chunks(feats, rv, gc, n_ctx)

        ffn = (hy_w1[l], hy_b1[l], hy_f1[l], hy_w2[l], hy_b2[l], hy_f2[l], hy_w3[l])
        hy_lat = _hyena_segment(hy, n_ctx, n_lat, hy_conv_w[l], hy_conv_b[l], ffn, hy_skip[l])
        hy_ctx = _hyena_segment(hy, 0, n_ctx, hy_conv_w[l], hy_conv_b[l], ffn, hy_skip[l]) if with_ctx else None

        x_all = _merge(x_all, mod[l], s5u, s5y, s5_d[l].reshape(-1), s5_glu_w[l], s5_glu_b[l], att,
                       rwf, rwb, bv, rw_ln_g[l], rw_ln_b[l], hy_lat, hy_ctx, gate_pre, branch_g[l], w_out_b[l],
                       seg_j, n_ctx, with_ctx)
    return x_all
```
